```python
import jax, jax.numpy as jnp
from jax import lax
import numpy as np


D_MODEL = 1024
BATCH = 16
SEQ = 4096
DEPTH = 2
DEC_BATCH = 2
DEC_SEQ = 16384
PAST_LEN = 128

GRID_W = 64
HEAD_DIM = 64
NA_HEADS = 8
RW_HEADS = 8
NA_WIDTH = NA_HEADS * HEAD_DIM
RW_WIDTH = RW_HEADS * HEAD_DIM
MIX_WIDTH = NA_WIDTH + RW_WIDTH
WIN_ROWS_MAX = 8
WIN_COLS = 16
DECAY_LORA = 64
AAA_LORA = 64
GATE_LORA = 128
N_DIR = 2
RW_COLS = 3 * RW_WIDTH + N_DIR * DECAY_LORA + N_DIR * AAA_LORA + GATE_LORA
PROJ_WIDTH = 3 * NA_WIDTH + RW_COLS
D_FF = 2816
PLE_DIM = 256
NORM_EPS = 1e-6
LNX_EPS = 64e-5
DECAY_SCALE = 0.606531

kernel_name = 'hybrid_natten_rwkv7_encoder'


def rms_norm(x, g):
    xf = x.astype(jnp.float32)
    y = xf * lax.rsqrt(jnp.mean(xf * xf, axis=-1, keepdims=True) + NORM_EPS)
    return (y * g.astype(jnp.float32)).astype(x.dtype)


def swiglu(x, wg, wu, wd):
    return (jax.nn.silu(x @ wg) * (x @ wu)) @ wd


def centred_shift_delta(u):
    prev = jnp.pad(u[:, :-1], ((0, 0), (1, 0), (0, 0)))
    nxt = jnp.pad(u[:, 1:], ((0, 0), (0, 1), (0, 0)))
    return 0.5 * (prev + nxt) - u


def neighbourhood_attention(q, k, v, rpb):
    B, T, _ = q.shape
    rows = T // GRID_W
    wr = min(WIN_ROWS_MAX, rows)
    scale = HEAD_DIM ** -0.5

    def grid(z):
        return z.reshape(B, rows, GRID_W, NA_HEADS, HEAD_DIM)

    qg = grid(q * scale)
    kg = grid(k)
    vg = grid(v)
    col = jnp.arange(GRID_W)
    col_start = jnp.clip(col - WIN_COLS // 2, 0, GRID_W - WIN_COLS)
    col_idx = col_start[:, None] + jnp.arange(WIN_COLS)[None, :]
    col_bias_idx = col_idx - col[:, None] + (WIN_COLS - 1)

    def one_row(args):
        i, q_row = args
        row_start = jnp.clip(i - wr // 2, 0, rows - wr)
        k_rows = lax.dynamic_slice_in_dim(kg, row_start, wr, axis=1)
        v_rows = lax.dynamic_slice_in_dim(vg, row_start, wr, axis=1)
        k_win = k_rows[:, :, col_idx]
        v_win = v_rows[:, :, col_idx]
        row_bias_idx = row_start + jnp.arange(wr) - i + (WIN_ROWS_MAX - 1)
        bias = rpb[:, row_bias_idx[:, None, None], col_bias_idx[None, :, :]]
        s = jnp.einsum('bjhd,bajchd->bhjac', q_row, k_win).astype(jnp.float32)
        s = s + jnp.transpose(bias, (0, 2, 1, 3)).astype(jnp.float32)[None]
        p = jax.nn.softmax(s.reshape(B, NA_HEADS, GRID_W, wr * WIN_COLS), axis=-1)
        p = p.reshape(B, NA_HEADS, GRID_W, wr, WIN_COLS).astype(v.dtype)
        return jnp.einsum('bhjac,bajchd->bjhd', p, v_win)

    o = lax.map(one_row, (jnp.arange(rows), jnp.moveaxis(qg, 1, 0)))
    return jnp.moveaxis(o, 0, 1).reshape(B, T, NA_WIDTH)


def wkv7_scan(r, w, k, v, a, b, reverse):
    T, B, H, N = r.shape

    def step(S, inp):
        r_t, w_t, k_t, v_t, a_t, b_t = inp
        sa = jnp.einsum('bhvk,bhk->bhv', S, a_t)
        S = S * w_t[:, :, None, :] + sa[..., None] * b_t[:, :, None, :] + v_t[..., None] * k_t[:, :, None, :]
        y = jnp.einsum('bhvk,bhk->bhv', S, r_t)
        return S, y

    S0 = jnp.zeros((B, H, N, N), jnp.float32)
    _, y = lax.scan(step, S0, (r, w, k, v, a, b), reverse=reverse)
    return y


def rwkv7_bidirectional(u, mu, w0, w_up, a0, a_up, g_up, k_k, k_a, r_k, lnx_w, lnx_b):
    B, T, _ = u.shape
    out_dtype = u.dtype
    u = (u + mu * centred_shift_delta(u)).astype(jnp.float32)
    r = u[..., 0:RW_WIDTH]
    k = u[..., RW_WIDTH:2 * RW_WIDTH]
    v = u[..., 2 * RW_WIDTH:3 * RW_WIDTH]
    lo = 3 * RW_WIDTH
    w_low = [u[..., lo + d * DECAY_LORA:lo + (d + 1) * DECAY_LORA] for d in range(N_DIR)]
    lo = lo + N_DIR * DECAY_LORA
    a_low = [u[..., lo + d * AAA_LORA:lo + (d + 1) * AAA_LORA] for d in range(N_DIR)]
    lo = lo + N_DIR * AAA_LORA
    g_low = u[..., lo:lo + GATE_LORA]

    def heads(z):
        return z.reshape(B, T, RW_HEADS, HEAD_DIM)

    def tmajor(z):
        return jnp.swapaxes(z, 0, 1)

    g = jax.nn.sigmoid(g_low) @ g_up.astype(jnp.float32)
    kk = heads(k * k_k)
    kk = kk / jnp.maximum(jnp.sqrt(jnp.sum(kk * kk, axis=-1, keepdims=True)), 1e-12)
    rh = heads(r)
    vh = heads(v)
    wkv = None
    bonus = None
    for d in range(N_DIR):
        z = w0[d] + jnp.tanh(w_low[d]) @ w_up[d]
        w = jnp.exp(-DECAY_SCALE * jax.nn.sigmoid(z.astype(jnp.float32)))
        a = jax.nn.sigmoid((a0[d] + a_low[d] @ a_up[d]).astype(jnp.float32))
        kd = heads(k * (1.0 + (a - 1.0) * k_a))
        ah = heads(a)
        y = wkv7_scan(tmajor(rh), tmajor(heads(w)), tmajor(kd), tmajor(vh), tmajor(-kk), tmajor(kk * ah), reverse=(d == 1))
        y = tmajor(y)
        bd = jnp.sum(rh * kd * r_k, axis=-1, keepdims=True) * vh
        wkv = y if wkv is None else wkv + y
        bonus = bd if bonus is None else bonus + bd
    mean = jnp.mean(wkv, axis=-1, keepdims=True)
    var = jnp.mean(jnp.square(wkv - mean), axis=-1, keepdims=True)
    yn = ((wkv - mean) * lax.rsqrt(var + LNX_EPS)).reshape(B, T, RW_WIDTH) * lnx_w + lnx_b
    out = (yn + bonus.reshape(B, T, RW_WIDTH)) * g
    return out.astype(out_dtype)


def encoder_trunk(x, p, w):
    h = x
    for i in range(DEPTH):
        h = h + 0.5 * swiglu(rms_norm(h, w['ffn1_norm'][i]), w['ffn1_wg'][i], w['ffn1_wu'][i], w['ffn1_wd'][i])
        n = rms_norm(h, w['mix_norm'][i])
        proj = n @ w['w_in'][i]
        q = proj[..., 0:NA_WIDTH]
        k = proj[..., NA_WIDTH:2 * NA_WIDTH]
        v = proj[..., 2 * NA_WIDTH:3 * NA_WIDTH]
        y_na = neighbourhood_attention(q, k, v, w['na_rpb'][i])
        y_rw = rwkv7_bidirectional(proj[..., 3 * NA_WIDTH:], w['rw_mu'][i], w['rw_w0'][i], w['rw_w_up'][i],
                                   w['rw_a0'][i], w['rw_a_up'][i], w['rw_g_up'][i], w['rw_k_k'][i],
                                   w['rw_k_a'][i], w['rw_r_k'][i], w['rw_lnx_w'][i], w['rw_lnx_b'][i])
        h = h + jnp.concatenate([y_na, y_rw], axis=-1) @ w['w_out'][i]
        h = h + 0.5 * swiglu(rms_norm(h, w['ffn2_norm'][i]), w['ffn2_wg'][i], w['ffn2_wu'][i], w['ffn2_wd'][i])
        gate = jax.nn.sigmoid(rms_norm(h, w['ple_norm'][i]) @ w['ple_gate'][i])
        h = h + gate * (p[i] @ w['ple_up'][i])
    return rms_norm(h, w['final_norm'])


def setup_inputs(seed: int = 0) -> dict:
    key = jax.random.key(seed)
    ks = jax.random.split(key, 40)
    f32 = jnp.float32

    def nrm(k, shape, scale):
        return jax.random.normal(k, shape, f32) * scale

    L = DEPTH
    return {
        'x_prompt': nrm(ks[0], (BATCH, SEQ, D_MODEL), 1.0),
        'x_sample': nrm(ks[1], (DEC_BATCH, DEC_SEQ, D_MODEL), 1.0),
        'p_prompt': nrm(ks[2], (DEPTH, BATCH, SEQ, PLE_DIM), 1.0),
        'p_sample': nrm(ks[3], (DEPTH, DEC_BATCH, DEC_SEQ, PLE_DIM), 1.0),
        'ffn1_norm': 1.0 + nrm(ks[4], (L, D_MODEL), 0.05),
        'ffn1_wg': nrm(ks[5], (L, D_MODEL, D_FF), D_MODEL ** -0.5),
        'ffn1_wu': nrm(ks[6], (L, D_MODEL, D_FF), D_MODEL ** -0.5),
        'ffn1_wd': nrm(ks[7], (L, D_FF, D_MODEL), D_FF ** -0.5),
        'mix_norm': 1.0 + nrm(ks[8], (L, D_MODEL), 0.05),
        'w_in': nrm(ks[9], (L, D_MODEL, PROJ_WIDTH), D_MODEL ** -0.5),
        'na_rpb': nrm(ks[10], (L, NA_HEADS, 2 * WIN_ROWS_MAX - 1, 2 * WIN_COLS - 1), 0.5),
        'rw_mu': jax.random.uniform(ks[11], (L, RW_COLS), f32),
        'rw_w0': -1.0 + nrm(ks[12], (L, N_DIR, RW_WIDTH), 1.5),
        'rw_w_up': nrm(ks[13], (L, N_DIR, DECAY_LORA, RW_WIDTH), 0.5 * DECAY_LORA ** -0.5),
        'rw_a0': nrm(ks[14], (L, N_DIR, RW_WIDTH), 0.5),
        'rw_a_up': nrm(ks[15], (L, N_DIR, AAA_LORA, RW_WIDTH), 0.5 * AAA_LORA ** -0.5),
        'rw_g_up': nrm(ks[16], (L, GATE_LORA, RW_WIDTH), GATE_LORA ** -0.5),
        'rw_k_k': 0.85 + nrm(ks[17], (L, RW_WIDTH), 0.05),
        'rw_k_a': 1.0 + nrm(ks[18], (L, RW_WIDTH), 0.05),
        'rw_r_k': nrm(ks[19], (L, RW_HEADS, HEAD_DIM), 0.1),
        'rw_lnx_w': 1.0 + nrm(ks[20], (L, RW_WIDTH), 0.05),
        'rw_lnx_b': nrm(ks[21], (L, RW_WIDTH), 0.01),
        'w_out': nrm(ks[22], (L, MIX_WIDTH, D_MODEL), MIX_WIDTH ** -0.5),
        'ffn2_norm': 1.0 + nrm(ks[23], (L, D_MODEL), 0.05),
        'ffn2_wg': nrm(ks[24], (L, D_MODEL, D_FF), D_MODEL ** -0.5),
        'ffn2_wu': nrm(ks[25], (L, D_MODEL, D_FF), D_MODEL ** -0.5),
        'ffn2_wd': nrm(ks[26], (L, D_FF, D_MODEL), D_FF ** -0.5),
        'ple_norm': 1.0 + nrm(ks[27], (L, D_MODEL), 0.05),
        'ple_gate': nrm(ks[28], (L, D_MODEL, D_MODEL), D_MODEL ** -0.5),
        'ple_up': nrm(ks[29], (L, PLE_DIM, D_MODEL), PLE_DIM ** -0.5),
        'final_norm': 1.0 + nrm(ks[30], (D_MODEL,), 0.05),
    }


def reference(x_prompt, x_sample, p_prompt, p_sample, ffn1_norm, ffn1_wg, ffn1_wu, ffn1_wd, mix_norm, w_in,
              na_rpb, rw_mu, rw_w0, rw_w_up, rw_a0, rw_a_up, rw_g_up, rw_k_k, rw_k_a, rw_r_k, rw_lnx_w,
              rw_lnx_b, w_out, ffn2_norm, ffn2_wg, ffn2_wu, ffn2_wd, ple_norm, ple_gate, ple_up, final_norm):
    weights = dict(ffn1_norm=ffn1_norm, ffn1_wg=ffn1_wg, ffn1_wu=ffn1_wu, ffn1_wd=ffn1_wd, mix_norm=mix_norm,
                   w_in=w_in, na_rpb=na_rpb, rw_mu=rw_mu, rw_w0=rw_w0, rw_w_up=rw_w_up, rw_a0=rw_a0,
                   rw_a_up=rw_a_up, rw_g_up=rw_g_up, rw_k_k=rw_k_k, rw_k_a=rw_k_a, rw_r_k=rw_r_k,
                   rw_lnx_w=rw_lnx_w, rw_lnx_b=rw_lnx_b, w_out=w_out, ffn2_norm=ffn2_norm, ffn2_wg=ffn2_wg,
                   ffn2_wu=ffn2_wu, ffn2_wd=ffn2_wd, ple_norm=ple_norm, ple_gate=ple_gate, ple_up=ple_up,
                   final_norm=final_norm)
    y_prompt = encoder_trunk(x_prompt, p_prompt, weights)
    y_sample = encoder_trunk(x_sample, p_sample, weights)
    return (y_prompt, y_sample)
```

```python
import functools

import jax
import jax.numpy as jnp
from jax import lax
from jax.experimental import pallas as pl
from jax.experimental.pallas import tpu as pltpu

F32 = jnp.float32
BF16 = jnp.bfloat16

GRID_W = 64
HEAD_DIM = 64
NA_HEADS = 8
RW_HEADS = 8
NA_WIDTH = NA_HEADS * HEAD_DIM
RW_WIDTH = RW_HEADS * HEAD_DIM
WIN_ROWS = 8
WIN_COLS = 16
DECAY_LORA = 64
AAA_LORA = 64
GATE_LORA = 128
RW_COLS = 3 * RW_WIDTH + 2 * DECAY_LORA + 2 * AAA_LORA + GATE_LORA
NORM_EPS = 1e-6
LNX_EPS = 64e-5
DECAY_SCALE = 0.606531
MASK_VALUE = -1e30

LANES = 128
MXU_DIM = 256
VMEM_LIMIT_BYTES = 52 * 1024 * 1024

ROW_TILE = 512
FF_TILE = 1408
NA_ROWS = 8
CHUNK = 64
WKV_STEP = 256
GROUP_HEADS = MXU_DIM // HEAD_DIM


def _params(*sem):
    return pltpu.CompilerParams(dimension_semantics=sem, vmem_limit_bytes=VMEM_LIMIT_BYTES)


def _bdot(a, b):
    return jnp.dot(a.astype(BF16), b.astype(BF16), preferred_element_type=F32)


def _bdot_nt(a, b):
    return lax.dot_general(a.astype(BF16), b.astype(BF16), (((1,), (1,)), ((), ())),
                           preferred_element_type=F32)


def _bdot_tn(a, b):
    return lax.dot_general(a.astype(BF16), b.astype(BF16), (((0,), (0,)), ((), ())),
                           preferred_element_type=F32)


def _split(x, n):
    parts = []
    rem = x
    for _ in range(n):
        p = rem.astype(BF16)
        parts.append(p)
        rem = rem - p.astype(F32)
    return parts


def _rms(x, g):
    ms = jnp.mean(x * x, axis=-1, keepdims=True)
    return x * lax.rsqrt(ms + NORM_EPS) * g


def _sigmoid(x):
    return 1.0 / (1.0 + jnp.exp(-x))


def _ffn_kernel(*refs, n_ff, with_ple, with_final):
    if with_ple:
        (x_ref, g_ref, wg_ref, wu_ref, wd_ref, p_ref, pn_ref, pg_ref, pu_ref, fn_ref,
         o_ref, xn_ref, acc_ref) = refs
    else:
        x_ref, g_ref, wg_ref, wu_ref, wd_ref, o_ref, xn_ref, acc_ref = refs
    j = pl.program_id(1)

    @pl.when(j == 0)
    def _():
        xn_ref[...] = _rms(x_ref[...], g_ref[...]).astype(BF16)
        acc_ref[...] = jnp.zeros_like(acc_ref)

    xn = xn_ref[...]
    a = jnp.dot(xn, wg_ref[...], preferred_element_type=F32)
    b = jnp.dot(xn, wu_ref[...], preferred_element_type=F32)
    mid = (a * _sigmoid(a)) * b
    acc_ref[...] += jnp.dot(mid.astype(BF16), wd_ref[...], preferred_element_type=F32)

    @pl.when(j == n_ff - 1)
    def _():
        h = x_ref[...] + 0.5 * acc_ref[...]
        if with_ple:
            gate = _sigmoid(_bdot(_rms(h, pn_ref[...]), pg_ref[...]))
            h = h + gate * _bdot(p_ref[...], pu_ref[...])
            if with_final:
                h = _rms(h, fn_ref[...])
        o_ref[...] = h


def _ffn(x, g, wg, wu, wd, ple=None, final_norm=None):
    m, d = x.shape
    dff = wg.shape[1]
    tm = min(ROW_TILE, m)
    tf = FF_TILE if dff % FF_TILE == 0 else dff
    n_ff = dff // tf
    row = lambda i, j: (i, 0)
    const = lambda i, j: (0, 0)
    in_specs = [
        pl.BlockSpec((tm, d), row),
        pl.BlockSpec((1, d), const),
        pl.BlockSpec((d, tf), lambda i, j: (0, j)),
        pl.BlockSpec((d, tf), lambda i, j: (0, j)),
        pl.BlockSpec((tf, d), lambda i, j: (j, 0)),
    ]
    args = [x, g, wg, wu, wd]
    if ple is not None:
        p, pn, pg, pu = ple
        fn = final_norm if final_norm is not None else pn
        in_specs += [
            pl.BlockSpec((tm, p.shape[1]), row),
            pl.BlockSpec((1, d), const),
            pl.BlockSpec((d, d), const),
            pl.BlockSpec((p.shape[1], d), const),
            pl.BlockSpec((1, d), const),
        ]
        args += [p, pn, pg, pu, fn]
    kern = functools.partial(_ffn_kernel, n_ff=n_ff, with_ple=ple is not None,
                             with_final=final_norm is not None)
    return pl.pallas_call(
        kern,
        grid=(m // tm, n_ff),
        in_specs=in_specs,
        out_specs=pl.BlockSpec((tm, d), row),
        out_shape=jax.ShapeDtypeStruct((m, d), F32),
        scratch_shapes=[pltpu.VMEM((tm, d), BF16), pltpu.VMEM((tm, d), F32)],
        compiler_params=_params("parallel", "arbitrary"),
        name="ffn",
    )(*args)


def _inproj_kernel(x_ref, g_ref, w_ref, qkv_ref, u_ref):
    xn = _rms(x_ref[...], g_ref[...]).astype(BF16)
    proj = jnp.dot(xn, w_ref[...], preferred_element_type=F32)
    na3 = 3 * NA_WIDTH
    lane = lax.broadcasted_iota(jnp.int32, (1, na3), 1)
    scale = jnp.where(lane < NA_WIDTH, HEAD_DIM ** -0.5, 1.0)
    qkv_ref[...] = (proj[:, :na3] * scale).astype(BF16)
    u_ref[...] = proj[:, na3:]


def _inproj(x, g, w):
    m, d = x.shape
    tm = min(ROW_TILE, m)
    na3 = 3 * NA_WIDTH
    return pl.pallas_call(
        _inproj_kernel,
        grid=(m // tm,),
        in_specs=[
            pl.BlockSpec((tm, d), lambda i: (i, 0)),
            pl.BlockSpec((1, d), lambda i: (0, 0)),
            pl.BlockSpec((d, na3 + RW_COLS), lambda i: (0, 0)),
        ],
        out_specs=[
            pl.BlockSpec((tm, na3), lambda i: (i, 0)),
            pl.BlockSpec((tm, RW_COLS), lambda i: (i, 0)),
        ],
        out_shape=[jax.ShapeDtypeStruct((m, na3), BF16), jax.ShapeDtypeStruct((m, RW_COLS), F32)],
        compiler_params=_params("parallel"),
        name="inproj",
    )(x, g, w)


def _na_bias_table(rpb):
    j = jnp.arange(GRID_W)
    c = jnp.arange(GRID_W)
    cs = jnp.clip(j - WIN_COLS // 2, 0, GRID_W - WIN_COLS)
    valid = (c[None, :] >= cs[:, None]) & (c[None, :] < cs[:, None] + WIN_COLS)
    cidx = jnp.clip(c[None, :] - j[:, None] + (WIN_COLS - 1), 0, 2 * WIN_COLS - 2)
    full = jnp.where(valid, rpb[:, :, cidx], MASK_VALUE)
    nri = 2 * WIN_ROWS - 2
    two = jnp.concatenate([full[:, 0:nri], full[:, 1:nri + 1]], axis=-1)
    two = two.reshape(NA_HEADS // 2, 2, nri, GRID_W, 2 * GRID_W)
    return jnp.transpose(two, (0, 2, 1, 3, 4)).reshape(NA_HEADS // 2, nri, 2 * GRID_W, 2 * GRID_W)


def _na_kernel(q_ref, kp_ref, kc_ref, kn_ref, vp_ref, vc_ref, vn_ref, tz_ref, o_ref,
               kbuf, vbuf, *, rows):
    qb = pl.program_id(1)
    blk = NA_ROWS * GRID_W
    kbuf[0:blk] = kp_ref[...]
    kbuf[blk:2 * blk] = kc_ref[...]
    kbuf[2 * blk:3 * blk] = kn_ref[...]
    vbuf[0:blk] = vp_ref[...]
    vbuf[blk:2 * blk] = vc_ref[...]
    vbuf[2 * blk:3 * blk] = vn_ref[...]
    lane = lax.broadcasted_iota(jnp.int32, (GRID_W, 2 * HEAD_DIM), 1)
    low = lane < HEAD_DIM
    win = WIN_ROWS * GRID_W

    def body(il, carry):
        i = qb * NA_ROWS + il
        rs = jnp.clip(i - WIN_ROWS // 2, 0, rows - WIN_ROWS)
        delta = i - rs
        off = pl.multiple_of((rs - qb * NA_ROWS + NA_ROWS) * GRID_W, GRID_W)
        qrow = q_ref[pl.ds(pl.multiple_of(il * GRID_W, GRID_W), GRID_W), :]
        kw = kbuf[pl.ds(off, win), :]
        vw = vbuf[pl.ds(off, win), :]
        outs = []
        for hp in range(NA_HEADS // 2):
            sl = slice(hp * 2 * HEAD_DIM, (hp + 1) * 2 * HEAD_DIM)
            qp = qrow[:, sl]
            zero = jnp.zeros_like(qp)
            qst = jnp.concatenate([jnp.where(low, qp, zero), jnp.where(low, zero, qp)], axis=0)
            s = lax.dot_general(qst, kw[:, sl], (((1,), (1,)), ((), ())),
                                preferred_element_type=F32)
            bias = jnp.concatenate(
                [tz_ref[hp, 2 * p - delta + (WIN_ROWS - 1)] for p in range(WIN_ROWS // 2)], axis=1)
            s = s + bias
            m = jnp.max(s, axis=1, keepdims=True)
            p = jnp.exp(s - m)
            l = jnp.sum(p, axis=1, keepdims=True)
            p = p * (1.0 / l)
            o = jnp.dot(p.astype(BF16), vw[:, sl], preferred_element_type=F32)
            outs.append(jnp.where(low, o[0:GRID_W], o[GRID_W:2 * GRID_W]))
        o_ref[pl.ds(pl.multiple_of(il * GRID_W, GRID_W), GRID_W), :] = (
            jnp.concatenate(outs, axis=1).astype(BF16))
        return carry

    lax.fori_loop(0, NA_ROWS, body, 0)


def _na(qkv, tz):
    b, t, _ = qkv.shape
    rows = t // GRID_W
    nqb = rows // NA_ROWS
    blk = NA_ROWS * GRID_W
    spec = lambda col, shift: pl.BlockSpec(
        (None, blk, NA_WIDTH),
        lambda bi, qi: (bi, jnp.clip(qi + shift, 0, nqb - 1), col))
    return pl.pallas_call(
        functools.partial(_na_kernel, rows=rows),
        grid=(b, nqb),
        in_specs=[spec(0, 0), spec(1, -1), spec(1, 0), spec(1, 1), spec(2, -1), spec(2, 0), spec(2, 1),
                  pl.BlockSpec(tz.shape, lambda bi, qi: (0, 0, 0, 0))],
        out_specs=pl.BlockSpec((None, blk, NA_WIDTH), lambda bi, qi: (bi, qi, 0)),
        out_shape=jax.ShapeDtypeStruct((b, t, NA_WIDTH), BF16),
        scratch_shapes=[pltpu.VMEM((3 * blk, NA_WIDTH), BF16), pltpu.VMEM((3 * blk, NA_WIDTH), BF16)],
        compiler_params=_params("parallel", "arbitrary"),
        name="natten",
    )(qkv, qkv, qkv, qkv, qkv, qkv, qkv, tz)


def _head_sum(x, hm):
    outs = []
    for q in range(RW_WIDTH // MXU_DIM):
        xs = x[:, q * MXU_DIM:(q + 1) * MXU_DIM]
        outs.append(sum(jnp.dot(p, hm, preferred_element_type=F32) for p in _split(xs, 2)))
    return jnp.concatenate(outs, axis=1)


def _stack(x, head_of_lane):
    xb = x.astype(BF16)
    zero = jnp.zeros_like(xb)
    return jnp.concatenate([jnp.where(head_of_lane == h, xb, zero) for h in range(GROUP_HEADS)], axis=0)


def _wkv_kernel(u_ref, up_ref, un_ref, mu_ref, w0_ref, wup_ref, a0_ref, aup_ref, gup_ref,
                kk_ref, ka_ref, rk_ref, hm_ref, y_ref, bon_ref, g_ref, zt_ref, *, reverse, nsteps):
    s_idx = pl.program_id(1)
    blk = (nsteps - 1 - s_idx) if reverse else s_idx
    ts = u_ref.shape[0]
    nchunk = ts // CHUNK
    ngroup = RW_WIDTH // MXU_DIM

    @pl.when(s_idx == 0)
    def _():
        zt_ref[...] = jnp.zeros_like(zt_ref)

    u = u_ref[...]
    row = lax.broadcasted_iota(jnp.int32, (ts, 1), 0)
    prev_row = jnp.where(blk > 0, up_ref[7:8, :], 0.0)
    next_row = jnp.where(blk < nsteps - 1, un_ref[0:1, :], 0.0)
    u_dn = jnp.where(row == 0, prev_row, pltpu.roll(u, 1, axis=0))
    u_up = jnp.where(row == ts - 1, next_row, pltpu.roll(u, ts - 1, axis=0))
    um = u + mu_ref[...] * (0.5 * (u_dn + u_up) - u)

    r = um[:, 0:RW_WIDTH]
    k = um[:, RW_WIDTH:2 * RW_WIDTH]
    v = um[:, 2 * RW_WIDTH:3 * RW_WIDTH]
    lo = 3 * RW_WIDTH
    wl = um[:, lo:lo + 2 * DECAY_LORA]
    al = um[:, lo + 2 * DECAY_LORA:lo + 2 * DECAY_LORA + 2 * AAA_LORA]
    gl = um[:, lo + 2 * DECAY_LORA + 2 * AAA_LORA:]

    hm = hm_ref[...]
    logw = -DECAY_SCALE * _sigmoid(w0_ref[...] + _bdot(jnp.tanh(wl), wup_ref[...]))
    a = _sigmoid(a0_ref[...] + _bdot(al, aup_ref[...]))
    kk = k * kk_ref[...]
    kk = kk / jnp.maximum(jnp.sqrt(_head_sum(kk * kk, hm)), 1e-12)
    kd = k * (1.0 + (a - 1.0) * ka_ref[...])
    bon_ref[...] = _head_sum(r * kd * rk_ref[...], hm) * v
    g_ref[...] = _bdot(_sigmoid(gl), gup_ref[...])
    bb = kk * a

    ti = lax.broadcasted_iota(jnp.int32, (CHUNK, CHUNK), 0)
    si = lax.broadcasted_iota(jnp.int32, (CHUNK, CHUNK), 1)
    tri = jnp.where((si >= ti) if reverse else (si <= ti), 1.0, 0.0).astype(BF16)
    lane = lax.broadcasted_iota(jnp.int32, (CHUNK, MXU_DIM), 1)
    trow = lax.broadcasted_iota(jnp.int32, (CHUNK, MXU_DIM), 0)
    head_of_lane = lane // HEAD_DIM
    s_of_lane = lane % HEAD_DIM
    if reverse:
        strict, incl = s_of_lane > trow, s_of_lane >= trow
    else:
        strict, incl = s_of_lane < trow, s_of_lane <= trow
    eye_lc = jnp.where(s_of_lane == trow, 1.0, 0.0)
    r256 = lax.broadcasted_iota(jnp.int32, (MXU_DIM, MXU_DIM), 0)
    c256 = lax.broadcasted_iota(jnp.int32, (MXU_DIM, MXU_DIM), 1)
    blockdiag = (r256 // HEAD_DIM) == (c256 // HEAD_DIM)
    eye256 = jnp.where(r256 == c256, 1.0, 0.0)

    chunk_terms = []
    for c in range(nchunk):
        rows = slice(c * CHUNK, (c + 1) * CHUNK)
        lw = logw[rows]
        cum = sum(jnp.dot(tri, p, preferred_element_type=F32) for p in _split(lw, 3))
        cmid = cum[CHUNK // 2:CHUNK // 2 + 1]
        ctot = cum[0:1] if reverse else cum[CHUNK - 1:CHUNK]
        e1 = jnp.exp(cum - cmid)
        e1i = jnp.exp(cmid - cum)
        ee = jnp.exp(ctot - cum)
        emid = jnp.exp(cmid)
        wtot = jnp.exp(ctot)
        r_g = r[rows] * e1
        a_g = -kk[rows] * (e1 * jnp.exp(-lw))
        b_g = bb[rows] * e1i
        k_g = kd[rows] * e1i
        r_s = r_g * emid
        a_s = a_g * emid
        b_e = bb[rows] * ee
        k_e = kd[rows] * ee
        vc = v[rows]
        per_group = []
        for q in range(ngroup):
            ls = slice(q * MXU_DIM, (q + 1) * MXU_DIM)
            v_st = _stack(vc[:, ls], head_of_lane)
            gram = _bdot_nt(
                jnp.concatenate([a_g[:, ls], r_g[:, ls]], axis=0),
                jnp.concatenate([_stack(b_g[:, ls], head_of_lane), _stack(k_g[:, ls], head_of_lane)],
                                axis=0))
            a_ab = jnp.where(strict, gram[0:CHUNK, 0:MXU_DIM], 0.0)
            a_ak = jnp.where(strict, gram[0:CHUNK, MXU_DIM:], 0.0)
            a_rb = jnp.where(incl, gram[CHUNK:, 0:MXU_DIM], 0.0)
            a_rk = jnp.where(incl, gram[CHUNK:, MXU_DIM:], 0.0)
            tinv = eye_lc + a_ab
            apow = _bdot(a_ab, _stack(a_ab, head_of_lane))
            npow = CHUNK.bit_length() - 2
            for jj in range(npow):
                ap_st = _stack(apow, head_of_lane)
                if jj < npow - 1:
                    both = _bdot(jnp.concatenate([apow, tinv], axis=0), ap_st)
                    apow = both[0:CHUNK]
                    tinv = tinv + both[CHUNK:]
                else:
                    tinv = tinv + _bdot(tinv, ap_st)
            xy = _bdot(jnp.concatenate([a_ak, a_rk], axis=0), v_st)
            au = _bdot(tinv, jnp.concatenate(
                [_stack(a_s[:, ls], head_of_lane), _stack(xy[0:CHUNK], head_of_lane)], axis=1))
            a_hat, u_hat = au[:, 0:MXU_DIM], au[:, MXU_DIM:]
            ry = _bdot(a_rb, jnp.concatenate(
                [_stack(a_hat, head_of_lane), _stack(u_hat, head_of_lane)], axis=1))
            r_hat = r_s[:, ls] + ry[:, 0:MXU_DIM]
            y_hat = ry[:, MXU_DIM:] + xy[CHUNK:]
            pt = jnp.where(blockdiag, _bdot_tn(a_hat, b_e[:, ls]), 0.0) + eye256 * wtot[:, ls]
            qt_full = _bdot_tn(jnp.concatenate([u_hat, vc[:, ls]], axis=0),
                               jnp.concatenate([b_e[:, ls], k_e[:, ls]], axis=0))
            qt_full = jnp.where(blockdiag, qt_full, 0.0)
            qt = sum(qt_full[h * HEAD_DIM:(h + 1) * HEAD_DIM] for h in range(GROUP_HEADS))
            per_group.append((r_hat, y_hat, pt, qt))
        chunk_terms.append(per_group)

    order = range(nchunk - 1, -1, -1) if reverse else range(nchunk)
    zts = [zt_ref[q] for q in range(ngroup)]
    for c in order:
        ys = []
        for q in range(ngroup):
            r_hat, y_hat, pt, qt = chunk_terms[c][q]
            zt = zts[q]
            ys.append(_bdot_nt(r_hat, _stack(zt, head_of_lane)) + y_hat)
            zts[q] = _bdot(zt, pt) + qt
        y_ref[c * CHUNK:(c + 1) * CHUNK, :] = jnp.concatenate(ys, axis=1)
    for q in range(ngroup):
        zt_ref[q] = zts[q]


def _wkv(u, wp, reverse):
    b, t, _ = u.shape
    ts = min(WKV_STEP, t)
    nsteps = t // ts
    hb = ts // 8
    nhb = t // 8
    pos = (lambda si: nsteps - 1 - si) if reverse else (lambda si: si)
    const2 = lambda bi, si: (0, 0)
    vec = pl.BlockSpec((1, RW_WIDTH), const2)
    out = pl.BlockSpec((None, ts, RW_WIDTH), lambda bi, si: (bi, pos(si), 0))
    shp = jax.ShapeDtypeStruct((b, t, RW_WIDTH), F32)
    return pl.pallas_call(
        functools.partial(_wkv_kernel, reverse=reverse, nsteps=nsteps),
        grid=(b, nsteps),
        in_specs=[
            pl.BlockSpec((None, ts, RW_COLS), lambda bi, si: (bi, pos(si), 0)),
            pl.BlockSpec((None, 8, RW_COLS), lambda bi, si: (bi, jnp.maximum(pos(si) * hb - 1, 0), 0)),
            pl.BlockSpec((None, 8, RW_COLS),
                         lambda bi, si: (bi, jnp.minimum((pos(si) + 1) * hb, nhb - 1), 0)),
            pl.BlockSpec((1, RW_COLS), const2),
            vec,
            pl.BlockSpec((2 * DECAY_LORA, RW_WIDTH), const2),
            vec,
            pl.BlockSpec((2 * AAA_LORA, RW_WIDTH), const2),
            pl.BlockSpec((GATE_LORA, RW_WIDTH), const2),
            vec, vec, vec,
            pl.BlockSpec((MXU_DIM, MXU_DIM), const2),
        ],
        out_specs=[out, out, out],
        out_shape=[shp, shp, shp],
        scratch_shapes=[pltpu.VMEM((RW_WIDTH // MXU_DIM, HEAD_DIM, MXU_DIM), F32)],
        compiler_params=_params("parallel", "arbitrary"),
        name="wkv_bwd" if reverse else "wkv_fwd",
    )(u, u, u, wp["mu"], wp["w0"], wp["wup"], wp["a0"], wp["aup"], wp["gup"],
      wp["k_k"], wp["k_a"], wp["r_k"], wp["hm"])


def _outproj_kernel(h_ref, yna_ref, y0_ref, y1_ref, b0_ref, b1_ref, g_ref, lw_ref, lb_ref, hm_ref,
                    w_ref, o_ref):
    hm = hm_ref[...]
    wkv = y0_ref[...] + y1_ref[...]
    mean = _head_sum(wkv, hm) * (1.0 / HEAD_DIM)
    dev = wkv - mean
    var = _head_sum(dev * dev, hm) * (1.0 / HEAD_DIM)
    yn = dev * lax.rsqrt(var + LNX_EPS) * lw_ref[...] + lb_ref[...]
    y_rw = (yn + b0_ref[...] + b1_ref[...]) * g_ref[...]
    mix = jnp.concatenate([yna_ref[...], y_rw.astype(BF16)], axis=1)
    o_ref[...] = h_ref[...] + jnp.dot(mix, w_ref[...], preferred_element_type=F32)


def _outproj(h, yna, y0, y1, b0, b1, g, lnx_w, lnx_b, hm, w_out):
    m, d = h.shape
    tm = min(ROW_TILE, m)
    row = lambda i: (i, 0)
    const = lambda i: (0, 0)
    tok = pl.BlockSpec((tm, RW_WIDTH), row)
    vec = pl.BlockSpec((1, RW_WIDTH), const)
    return pl.pallas_call(
        _outproj_kernel,
        grid=(m // tm,),
        in_specs=[pl.BlockSpec((tm, d), row), pl.BlockSpec((tm, NA_WIDTH), row), tok, tok, tok, tok, tok,
                  vec, vec, pl.BlockSpec((MXU_DIM, MXU_DIM), const),
                  pl.BlockSpec((NA_WIDTH + RW_WIDTH, d), const)],
        out_specs=pl.BlockSpec((tm, d), row),
        out_shape=jax.ShapeDtypeStruct((m, d), F32),
        compiler_params=_params("parallel"),
        name="outproj",
    )(h, yna, y0, y1, b0, b1, g, lnx_w, lnx_b, hm, w_out)


def _pad_lora(w_up):
    zero = jnp.zeros_like(w_up[0])
    return (jnp.concatenate([w_up[0], zero], axis=0).astype(BF16),
            jnp.concatenate([zero, w_up[1]], axis=0).astype(BF16))


def _layer_weights(w, i):
    row = lambda x: x.reshape(1, -1)
    wup = _pad_lora(w["rw_w_up"][i])
    aup = _pad_lora(w["rw_a_up"][i])
    lane = jnp.arange(MXU_DIM)
    hm = (lane[:, None] // HEAD_DIM == lane[None, :] // HEAD_DIM).astype(BF16)
    common = dict(mu=row(w["rw_mu"][i]), gup=w["rw_g_up"][i].astype(BF16), k_k=row(w["rw_k_k"][i]),
                  k_a=row(w["rw_k_a"][i]), r_k=row(w["rw_r_k"][i]), hm=hm)
    dirs = [dict(common, w0=row(w["rw_w0"][i, d]), wup=wup[d], a0=row(w["rw_a0"][i, d]), aup=aup[d])
            for d in range(2)]
    return dict(
        ffn1=(row(w["ffn1_norm"][i]), w["ffn1_wg"][i].astype(BF16), w["ffn1_wu"][i].astype(BF16),
              w["ffn1_wd"][i].astype(BF16)),
        ffn2=(row(w["ffn2_norm"][i]), w["ffn2_wg"][i].astype(BF16), w["ffn2_wu"][i].astype(BF16),
              w["ffn2_wd"][i].astype(BF16)),
        mix_norm=row(w["mix_norm"][i]), w_in=w["w_in"][i].astype(BF16),
        tz=_na_bias_table(w["na_rpb"][i]), dirs=dirs, hm=hm,
        lnx_w=row(w["rw_lnx_w"][i]), lnx_b=row(w["rw_lnx_b"][i]), w_out=w["w_out"][i].astype(BF16),
        ple=(row(w["ple_norm"][i]), w["ple_gate"][i].astype(BF16), w["ple_up"][i].astype(BF16)),
    )


def _trunk(x, p, layers, final_norm):
    b, t, d = x.shape
    m = b * t
    h = x.reshape(m, d)
    depth = len(layers)
    for i, lw in enumerate(layers):
        h = _ffn(h, *lw["ffn1"])
        qkv, u = _inproj(h, lw["mix_norm"], lw["w_in"])
        yna = _na(qkv.reshape(b, t, -1), lw["tz"])
        u3 = u.reshape(b, t, -1)
        y0, b0, g = _wkv(u3, lw["dirs"][0], reverse=False)
        y1, b1, _ = _wkv(u3, lw["dirs"][1], reverse=True)
        flat = lambda z: z.reshape(m, -1)
        h = _outproj(h, flat(yna), flat(y0), flat(y1), flat(b0), flat(b1), flat(g),
                     lw["lnx_w"], lw["lnx_b"], lw["hm"], lw["w_out"])
        pn, pg, pu = lw["ple"]
        h = _ffn(h, *lw["ffn2"], ple=(p[i].reshape(m, -1), pn, pg, pu),
                 final_norm=final_norm if i == depth - 1 else None)
    return h.reshape(b, t, d)


def kernel(x_prompt, x_sample, p_prompt, p_sample, ffn1_norm, ffn1_wg, ffn1_wu, ffn1_wd, mix_norm, w_in, na_rpb, rw_mu, rw_w0, rw_w_up, rw_a0, rw_a_up, rw_g_up, rw_k_k, rw_k_a, rw_r_k, rw_lnx_w, rw_lnx_b, w_out, ffn2_norm, ffn2_wg, ffn2_wu, ffn2_wd, ple_norm, ple_gate, ple_up, final_norm):
    w = dict(ffn1_norm=ffn1_norm, ffn1_wg=ffn1_wg, ffn1_wu=ffn1_wu, ffn1_wd=ffn1_wd, mix_norm=mix_norm,
             w_in=w_in, na_rpb=na_rpb, rw_mu=rw_mu, rw_w0=rw_w0, rw_w_up=rw_w_up, rw_a0=rw_a0,
             rw_a_up=rw_a_up, rw_g_up=rw_g_up, rw_k_k=rw_k_k, rw_k_a=rw_k_a, rw_r_k=rw_r_k,
             rw_lnx_w=rw_lnx_w, rw_lnx_b=rw_lnx_b, w_out=w_out, ffn2_norm=ffn2_norm, ffn2_wg=ffn2_wg,
             ffn2_wu=ffn2_wu, ffn2_wd=ffn2_wd, ple_norm=ple_norm, ple_gate=ple_gate, ple_up=ple_up)
    layers = [_layer_weights(w, i) for i in range(ffn1_wg.shape[0])]
    fn = final_norm.reshape(1, -1)
    return (_trunk(x_prompt, p_prompt, layers, fn), _trunk(x_sample, p_sample, layers, fn))
```

```python
import functools

import jax
import jax.numpy as jnp
from jax import lax
from jax.experimental import pallas as pl
from jax.experimental.pallas import tpu as pltpu

F32 = jnp.float32
BF16 = jnp.bfloat16

GRID_W = 64
HEAD_DIM = 64
NA_HEADS = 8
RW_HEADS = 8
NA_WIDTH = NA_HEADS * HEAD_DIM
RW_WIDTH = RW_HEADS * HEAD_DIM
WIN_ROWS = 8
WIN_COLS = 16
DECAY_LORA = 64
AAA_LORA = 64
GATE_LORA = 128
RW_COLS = 3 * RW_WIDTH + 2 * DECAY_LORA + 2 * AAA_LORA + GATE_LORA
NORM_EPS = 1e-6
LNX_EPS = 64e-5
DECAY_SCALE = 0.606531
MASK_VALUE = -1e30

LANES = 128
MXU_DIM = 256
VMEM_LIMIT_BYTES = 52 * 1024 * 1024

ROW_TILE = 512
FF_TILE = 1408
NA_ROWS = 8
CHUNK = 64
WKV_STEP = 256
GROUP_HEADS = MXU_DIM // HEAD_DIM


def _params(*sem):
    return pltpu.CompilerParams(dimension_semantics=sem, vmem_limit_bytes=VMEM_LIMIT_BYTES)


def _bdot(a, b):
    return jnp.dot(a.astype(BF16), b.astype(BF16), preferred_element_type=F32)


def _bdot_nt(a, b):
    return lax.dot_general(a.astype(BF16), b.astype(BF16), (((1,), (1,)), ((), ())),
                           preferred_element_type=F32)


def _bdot_tn(a, b):
    return lax.dot_general(a.astype(BF16), b.astype(BF16), (((0,), (0,)), ((), ())),
                           preferred_element_type=F32)


def _split(x, n):
    parts = []
    rem = x
    for _ in range(n):
        p = rem.astype(BF16)
        parts.append(p)
        rem = rem - p.astype(F32)
    return parts


def _rms(x, g):
    ms = jnp.mean(x * x, axis=-1, keepdims=True)
    return x * lax.rsqrt(ms + NORM_EPS) * g


def _sigmoid(x):
    return 1.0 / (1.0 + jnp.exp(-x))


def _ffn_kernel(*refs, n_ff, with_ple, with_final):
    if with_ple:
        (x_ref, g_ref, wg_ref, wu_ref, wd_ref, p_ref, pn_ref, pg_ref, pu_ref, fn_ref,
         o_ref, xn_ref, acc_ref) = refs
    else:
        x_ref, g_ref, wg_ref, wu_ref, wd_ref, o_ref, xn_ref, acc_ref = refs
    j = pl.program_id(1)

    @pl.when(j == 0)
    def _():
        xn_ref[...] = _rms(x_ref[...], g_ref[...]).astype(BF16)
        acc_ref[...] = jnp.zeros_like(acc_ref)

    xn = xn_ref[...]
    a = jnp.dot(xn, wg_ref[...], preferred_element_type=F32)
    b = jnp.dot(xn, wu_ref[...], preferred_element_type=F32)
    mid = (a * _sigmoid(a)) * b
    acc_ref[...] += jnp.dot(mid.astype(BF16), wd_ref[...], preferred_element_type=F32)

    @pl.when(j == n_ff - 1)
    def _():
        h = x_ref[...] + 0.5 * acc_ref[...]
        if with_ple:
            gate = _sigmoid(_bdot(_rms(h, pn_ref[...]), pg_ref[...]))
            h = h + gate * _bdot(p_ref[...], pu_ref[...])
            if with_final:
                h = _rms(h, fn_ref[...])
        o_ref[...] = h


def _ffn(x, g, wg, wu, wd, ple=None, final_norm=None):
    m, d = x.shape
    dff = wg.shape[1]
    tm = min(ROW_TILE, m)
    tf = FF_TILE if dff % FF_TILE == 0 else dff
    n_ff = dff // tf
    row = lambda i, j: (i, 0)
    const = lambda i, j: (0, 0)
    in_specs = [
        pl.BlockSpec((tm, d), row),
        pl.BlockSpec((1, d), const),
        pl.BlockSpec((d, tf), lambda i, j: (0, j)),
        pl.BlockSpec((d, tf), lambda i, j: (0, j)),
        pl.BlockSpec((tf, d), lambda i, j: (j, 0)),
    ]
    args = [x, g, wg, wu, wd]
    if ple is not None:
        p, pn, pg, pu = ple
        fn = final_norm if final_norm is not None else pn
        in_specs += [
            pl.BlockSpec((tm, p.shape[1]), row),
            pl.BlockSpec((1, d), const),
            pl.BlockSpec((d, d), const),
            pl.BlockSpec((p.shape[1], d), const),
            pl.BlockSpec((1, d), const),
        ]
        args += [p, pn, pg, pu, fn]
    kern = functools.partial(_ffn_kernel, n_ff=n_ff, with_ple=ple is not None,
                             with_final=final_norm is not None)
    return pl.pallas_call(
        kern,
        grid=(m // tm, n_ff),
        in_specs=in_specs,
        out_specs=pl.BlockSpec((tm, d), row),
        out_shape=jax.ShapeDtypeStruct((m, d), F32),
        scratch_shapes=[pltpu.VMEM((tm, d), BF16), pltpu.VMEM((tm, d), F32)],
        compiler_params=_params("parallel", "arbitrary"),
        name="ffn",
    )(*args)


def _inproj_kernel(x_ref, g_ref, w_ref, qkv_ref, u_ref):
    xn = _rms(x_ref[...], g_ref[...]).astype(BF16)
    proj = jnp.dot(xn, w_ref[...], preferred_element_type=F32)
    na3 = 3 * NA_WIDTH
    lane = lax.broadcasted_iota(jnp.int32, (1, na3), 1)
    scale = jnp.where(lane < NA_WIDTH, HEAD_DIM ** -0.5, 1.0)
    qkv_ref[...] = (proj[:, :na3] * scale).astype(BF16)
    u_ref[...] = proj[:, na3:]


def _inproj(x, g, w):
    m, d = x.shape
    tm = min(ROW_TILE, m)
    na3 = 3 * NA_WIDTH
    return pl.pallas_call(
        _inproj_kernel,
        grid=(m // tm,),
        in_specs=[
            pl.BlockSpec((tm, d), lambda i: (i, 0)),
            pl.BlockSpec((1, d), lambda i: (0, 0)),
            pl.BlockSpec((d, na3 + RW_COLS), lambda i: (0, 0)),
        ],
        out_specs=[
            pl.BlockSpec((tm, na3), lambda i: (i, 0)),
            pl.BlockSpec((tm, RW_COLS), lambda i: (i, 0)),
        ],
        out_shape=[jax.ShapeDtypeStruct((m, na3), BF16), jax.ShapeDtypeStruct((m, RW_COLS), F32)],
        compiler_params=_params("parallel"),
        name="inproj",
    )(x, g, w)


def _na_bias_table(rpb):
    j = jnp.arange(GRID_W)
    c = jnp.arange(GRID_W)
    cs = jnp.clip(j - WIN_COLS // 2, 0, GRID_W - WIN_COLS)
    valid = (c[None, :] >= cs[:, None]) & (c[None, :] < cs[:, None] + WIN_COLS)
    cidx = jnp.clip(c[None, :] - j[:, None] + (WIN_COLS - 1), 0, 2 * WIN_COLS - 2)
    full = jnp.where(valid, rpb[:, :, cidx], MASK_VALUE)
    nri = 2 * WIN_ROWS - 2
    two = jnp.concatenate([full[:, 0:nri], full[:, 1:nri + 1]], axis=-1)
    two = two.reshape(NA_HEADS // 2, 2, nri, GRID_W, 2 * GRID_W)
    return jnp.transpose(two, (0, 2, 1, 3, 4)).reshape(NA_HEADS // 2, nri, 2 * GRID_W, 2 * GRID_W)


def _na_kernel(q_ref, kp_ref, kc_ref, kn_ref, vp_ref, vc_ref, vn_ref, tz_ref, o_ref,
               kbuf, vbuf, *, rows):
    qb = pl.program_id(1)
    blk = NA_ROWS * GRID_W
    kbuf[0:blk] = kp_ref[...]
    kbuf[blk:2 * blk] = kc_ref[...]
    kbuf[2 * blk:3 * blk] = kn_ref[...]
    vbuf[0:blk] = vp_ref[...]
    vbuf[blk:2 * blk] = vc_ref[...]
    vbuf[2 * blk:3 * blk] = vn_ref[...]
    lane = lax.broadcasted_iota(jnp.int32, (GRID_W, 2 * HEAD_DIM), 1)
    low = lane < HEAD_DIM
    win = WIN_ROWS * GRID_W

    def body(il, carry):
        i = qb * NA_ROWS + il
        rs = jnp.clip(i - WIN_ROWS // 2, 0, rows - WIN_ROWS)
        delta = i - rs
        off = pl.multiple_of((rs - qb * NA_ROWS + NA_ROWS) * GRID_W, GRID_W)
        qrow = q_ref[pl.ds(pl.multiple_of(il * GRID_W, GRID_W), GRID_W), :]
        kw = kbuf[pl.ds(off, win), :]
        vw = vbuf[pl.ds(off, win), :]
        outs = []
        for hp in range(NA_HEADS // 2):
            sl = slice(hp * 2 * HEAD_DIM, (hp + 1) * 2 * HEAD_DIM)
            qp = qrow[:, sl]
            zero = jnp.zeros_like(qp)
            qst = jnp.concatenate([jnp.where(low, qp, zero), jnp.where(low, zero, qp)], axis=0)
            s = lax.dot_general(qst, kw[:, sl], (((1,), (1,)), ((), ())),
                                preferred_element_type=F32)
            bias = jnp.concatenate(
                [tz_ref[hp, 2 * p - delta + (WIN_ROWS - 1)] for p in range(WIN_ROWS // 2)], axis=1)
            s = s + bias
            m = jnp.max(s, axis=1, keepdims=True)
            p = jnp.exp(s - m)
            l = jnp.sum(p, axis=1, keepdims=True)
            p = p * (1.0 / l)
            o = jnp.dot(p.astype(BF16), vw[:, sl], preferred_element_type=F32)
            outs.append(jnp.where(low, o[0:GRID_W], o[GRID_W:2 * GRID_W]))
        o_ref[pl.ds(pl.multiple_of(il * GRID_W, GRID_W), GRID_W), :] = (
            jnp.concatenate(outs, axis=1).astype(BF16))
        return carry

    lax.fori_loop(0, NA_ROWS, body, 0)


def _na(qkv, tz):
    b, t, _ = qkv.shape
    rows = t // GRID_W
    nqb = rows // NA_ROWS
    blk = NA_ROWS * GRID_W
    spec = lambda col, shift: pl.BlockSpec(
        (None, blk, NA_WIDTH),
        lambda bi, qi: (bi, jnp.clip(qi + shift, 0, nqb - 1), col))
    return pl.pallas_call(
        functools.partial(_na_kernel, rows=rows),
        grid=(b, nqb),
        in_specs=[spec(0, 0), spec(1, -1), spec(1, 0), spec(1, 1), spec(2, -1), spec(2, 0), spec(2, 1),
                  pl.BlockSpec(tz.shape, lambda bi, qi: (0, 0, 0, 0))],
        out_specs=pl.BlockSpec((None, blk, NA_WIDTH), lambda bi, qi: (bi, qi, 0)),
        out_shape=jax.ShapeDtypeStruct((b, t, NA_WIDTH), BF16),
        scratch_shapes=[pltpu.VMEM((3 * blk, NA_WIDTH), BF16), pltpu.VMEM((3 * blk, NA_WIDTH), BF16)],
        compiler_params=_params("parallel", "arbitrary"),
        name="natten",
    )(qkv, qkv, qkv, qkv, qkv, qkv, qkv, tz)


def _head_sum(x, hm):
    outs = []
    for q in range(RW_WIDTH // MXU_DIM):
        xs = x[:, q * MXU_DIM:(q + 1) * MXU_DIM]
        outs.append(sum(jnp.dot(p, hm, preferred_element_type=F32) for p in _split(xs, 2)))
    return jnp.concatenate(outs, axis=1)


def _stack(x, head_of_lane):
    xb = x.astype(BF16)
    zero = jnp.zeros_like(xb)
    return jnp.concatenate([jnp.where(head_of_lane == h, xb, zero) for h in range(GROUP_HEADS)], axis=0)


def _wkv_kernel(u_ref, up_ref, un_ref, mu_ref, w0_ref, wup_ref, a0_ref, aup_ref, gup_ref,
                kk_ref, ka_ref, rk_ref, hm_ref, y_ref, bon_ref, g_ref, zt_ref, *, reverse, nsteps):
    s_idx = pl.program_id(1)
    blk = (nsteps - 1 - s_idx) if reverse else s_idx
    ts = u_ref.shape[0]
    nchunk = ts // CHUNK
    ngroup = RW_WIDTH // MXU_DIM

    @pl.when(s_idx == 0)
    def _():
        zt_ref[...] = jnp.zeros_like(zt_ref)

    u = u_ref[...]
    row = lax.broadcasted_iota(jnp.int32, (ts, 1), 0)
    prev_row = jnp.where(blk > 0, up_ref[7:8, :], 0.0)
    next_row = jnp.where(blk < nsteps - 1, un_ref[0:1, :], 0.0)
    u_dn = jnp.where(row == 0, prev_row, pltpu.roll(u, 1, axis=0))
    u_up = jnp.where(row == ts - 1, next_row, pltpu.roll(u, ts - 1, axis=0))
    um = u + mu_ref[...] * (0.5 * (u_dn + u_up) - u)

    r = um[:, 0:RW_WIDTH]
    k = um[:, RW_WIDTH:2 * RW_WIDTH]
    v = um[:, 2 * RW_WIDTH:3 * RW_WIDTH]
    lo = 3 * RW_WIDTH
    wl = um[:, lo:lo + 2 * DECAY_LORA]
    al = um[:, lo + 2 * DECAY_LORA:lo + 2 * DECAY_LORA + 2 * AAA_LORA]
    gl = um[:, lo + 2 * DECAY_LORA + 2 * AAA_LORA:]

    hm = hm_ref[...]
    logw = -DECAY_SCALE * _sigmoid(w0_ref[...] + _bdot(jnp.tanh(wl), wup_ref[...]))
    a = _sigmoid(a0_ref[...] + _bdot(al, aup_ref[...]))
    kk = k * kk_ref[...]
    kk = kk / jnp.maximum(jnp.sqrt(_head_sum(kk * kk, hm)), 1e-12)
    kd = k * (1.0 + (a - 1.0) * ka_ref[...])
    bon_ref[...] = _head_sum(r * kd * rk_ref[...], hm) * v
    g_ref[...] = _bdot(_sigmoid(gl), gup_ref[...])
    bb = kk * a

    ti = lax.broadcasted_iota(jnp.int32, (CHUNK, CHUNK), 0)
    si = lax.broadcasted_iota(jnp.int32, (CHUNK, CHUNK), 1)
    tri = jnp.where((si >= ti) if reverse else (si <= ti), 1.0, 0.0).astype(BF16)
    lane = lax.broadcasted_iota(jnp.int32, (CHUNK, MXU_DIM), 1)
    trow = lax.broadcasted_iota(jnp.int32, (CHUNK, MXU_DIM), 0)
    head_of_lane = lane // HEAD_DIM
    s_of_lane = lane % HEAD_DIM
    if reverse:
        strict, incl = s_of_lane > trow, s_of_lane >= trow
    else:
        strict, incl = s_of_lane < trow, s_of_lane <= trow
    eye_lc = jnp.where(s_of_lane == trow, 1.0, 0.0)
    r256 = lax.broadcasted_iota(jnp.int32, (MXU_DIM, MXU_DIM), 0)
    c256 = lax.broadcasted_iota(jnp.int32, (MXU_DIM, MXU_DIM), 1)
    blockdiag = (r256 // HEAD_DIM) == (c256 // HEAD_DIM)
    eye256 = jnp.where(r256 == c256, 1.0, 0.0)

    chains = [(c, q) for c in range(nchunk) for q in range(ngroup)]
    lanes = [slice(q * MXU_DIM, (q + 1) * MXU_DIM) for q in range(ngroup)]
    rows = [slice(c * CHUNK, (c + 1) * CHUNK) for c in range(nchunk)]
    st = lambda x: _stack(x, head_of_lane)

    cums = [sum(jnp.dot(tri, p, preferred_element_type=F32) for p in _split(logw[rows[c]], 3))
            for c in range(nchunk)]
    dec = []
    for c in range(nchunk):
        cum, lw = cums[c], logw[rows[c]]
        cmid = cum[CHUNK // 2:CHUNK // 2 + 1]
        ctot = cum[0:1] if reverse else cum[CHUNK - 1:CHUNK]
        e1 = jnp.exp(cum - cmid)
        e1i = jnp.exp(cmid - cum)
        ee = jnp.exp(ctot - cum)
        emid = jnp.exp(cmid)
        r_g = r[rows[c]] * e1
        a_g = -kk[rows[c]] * (e1 * jnp.exp(-lw))
        dec.append(dict(
            r_g=r_g, a_g=a_g, b_g=bb[rows[c]] * e1i, k_g=kd[rows[c]] * e1i, r_s=r_g * emid,
            a_s=a_g * emid, b_e=bb[rows[c]] * ee, k_e=kd[rows[c]] * ee, v=v[rows[c]],
            wtot=jnp.exp(ctot)))

    grams = [_bdot_nt(jnp.concatenate([dec[c]["a_g"][:, lanes[q]], dec[c]["r_g"][:, lanes[q]]], axis=0),
                      jnp.concatenate([st(dec[c]["b_g"][:, lanes[q]]), st(dec[c]["k_g"][:, lanes[q]])],
                                      axis=0)) for c, q in chains]
    a_ab = [jnp.where(strict, g[0:CHUNK, 0:MXU_DIM], 0.0) for g in grams]
    a_kr = [jnp.concatenate([jnp.where(strict, g[0:CHUNK, MXU_DIM:], 0.0),
                             jnp.where(incl, g[CHUNK:, MXU_DIM:], 0.0)], axis=0) for g in grams]
    a_rb = [jnp.where(incl, g[CHUNK:, 0:MXU_DIM], 0.0) for g in grams]
    tinv = [eye_lc + a for a in a_ab]
    apow = [_bdot(a, st(a)) for a in a_ab]
    npow = CHUNK.bit_length() - 2
    for jj in range(npow):
        if jj < npow - 1:
            both = [_bdot(jnp.concatenate([ap, ti], axis=0), st(ap)) for ap, ti in zip(apow, tinv)]
            apow = [b[0:CHUNK] for b in both]
            tinv = [ti + b[CHUNK:] for ti, b in zip(tinv, both)]
        else:
            tinv = [ti + _bdot(ti, st(ap)) for ap, ti in zip(apow, tinv)]
    xy = [_bdot(a_kr[i], st(dec[c]["v"][:, lanes[q]])) for i, (c, q) in enumerate(chains)]
    au = [_bdot(tinv[i], jnp.concatenate([st(dec[c]["a_s"][:, lanes[q]]), st(xy[i][0:CHUNK])], axis=1))
          for i, (c, q) in enumerate(chains)]
    ry = [_bdot(a_rb[i], jnp.concatenate([st(au[i][:, 0:MXU_DIM]), st(au[i][:, MXU_DIM:])], axis=1))
          for i in range(len(chains))]
    pts = [_bdot_tn(au[i][:, 0:MXU_DIM], dec[c]["b_e"][:, lanes[q]]) for i, (c, q) in enumerate(chains)]
    qts = [_bdot_tn(jnp.concatenate([au[i][:, MXU_DIM:], dec[c]["v"][:, lanes[q]]], axis=0),
                    jnp.concatenate([dec[c]["b_e"][:, lanes[q]], dec[c]["k_e"][:, lanes[q]]], axis=0))
           for i, (c, q) in enumerate(chains)]
    chunk_terms = [[None] * ngroup for _ in range(nchunk)]
    for i, (c, q) in enumerate(chains):
        r_hat = dec[c]["r_s"][:, lanes[q]] + ry[i][:, 0:MXU_DIM]
        y_hat = ry[i][:, MXU_DIM:] + xy[i][CHUNK:]
        pt = jnp.where(blockdiag, pts[i], 0.0) + eye256 * dec[c]["wtot"][:, lanes[q]]
        qt_full = jnp.where(blockdiag, qts[i], 0.0)
        qt = sum(qt_full[h * HEAD_DIM:(h + 1) * HEAD_DIM] for h in range(GROUP_HEADS))
        chunk_terms[c][q] = (r_hat, y_hat, pt, qt)


    order = range(nchunk - 1, -1, -1) if reverse else range(nchunk)
    zts = [zt_ref[q] for q in range(ngroup)]
    for c in order:
        ys = []
        for q in range(ngroup):
            r_hat, y_hat, pt, qt = chunk_terms[c][q]
            zt = zts[q]
            ys.append(_bdot_nt(r_hat, _stack(zt, head_of_lane)) + y_hat)
            zts[q] = _bdot(zt, pt) + qt
        y_ref[c * CHUNK:(c + 1) * CHUNK, :] = jnp.concatenate(ys, axis=1)
    for q in range(ngroup):
        zt_ref[q] = zts[q]


def _wkv(u, wp, reverse):
    b, t, _ = u.shape
    ts = min(WKV_STEP, t)
    nsteps = t // ts
    hb = ts // 8
    nhb = t // 8
    pos = (lambda si: nsteps - 1 - si) if reverse else (lambda si: si)
    const2 = lambda bi, si: (0, 0)
    vec = pl.BlockSpec((1, RW_WIDTH), const2)
    out = pl.BlockSpec((None, ts, RW_WIDTH), lambda bi, si: (bi, pos(si), 0))
    shp = jax.ShapeDtypeStruct((b, t, RW_WIDTH), F32)
    return pl.pallas_call(
        functools.partial(_wkv_kernel, reverse=reverse, nsteps=nsteps),
        grid=(b, nsteps),
        in_specs=[
            pl.BlockSpec((None, ts, RW_COLS), lambda bi, si: (bi, pos(si), 0)),
            pl.BlockSpec((None, 8, RW_COLS), lambda bi, si: (bi, jnp.maximum(pos(si) * hb - 1, 0), 0)),
            pl.BlockSpec((None, 8, RW_COLS),
                         lambda bi, si: (bi, jnp.minimum((pos(si) + 1) * hb, nhb - 1), 0)),
            pl.BlockSpec((1, RW_COLS), const2),
            vec,
            pl.BlockSpec((2 * DECAY_LORA, RW_WIDTH), const2),
            vec,
            pl.BlockSpec((2 * AAA_LORA, RW_WIDTH), const2),
            pl.BlockSpec((GATE_LORA, RW_WIDTH), const2),
            vec, vec, vec,
            pl.BlockSpec((MXU_DIM, MXU_DIM), const2),
        ],
        out_specs=[out, out, out],
        out_shape=[shp, shp, shp],
        scratch_shapes=[pltpu.VMEM((RW_WIDTH // MXU_DIM, HEAD_DIM, MXU_DIM), F32)],
        compiler_params=_params("parallel", "arbitrary"),
        name="wkv_bwd" if reverse else "wkv_fwd",
    )(u, u, u, wp["mu"], wp["w0"], wp["wup"], wp["a0"], wp["aup"], wp["gup"],
      wp["k_k"], wp["k_a"], wp["r_k"], wp["hm"])


def _outproj_kernel(h_ref, yna_ref, y0_ref, y1_ref, b0_ref, b1_ref, g_ref, lw_ref, lb_ref, hm_ref,
                    w_ref, o_ref):
    hm = hm_ref[...]
    wkv = y0_ref[...] + y1_ref[...]
    mean = _head_sum(wkv, hm) * (1.0 / HEAD_DIM)
    dev = wkv - mean
    var = _head_sum(dev * dev, hm) * (1.0 / HEAD_DIM)
    yn = dev * lax.rsqrt(var + LNX_EPS) * lw_ref[...] + lb_ref[...]
    y_rw = (yn + b0_ref[...] + b1_ref[...]) * g_ref[...]
    mix = jnp.concatenate([yna_ref[...], y_rw.astype(BF16)], axis=1)
    o_ref[...] = h_ref[...] + jnp.dot(mix, w_ref[...], preferred_element_type=F32)


def _outproj(h, yna, y0, y1, b0, b1, g, lnx_w, lnx_b, hm, w_out):
    m, d = h.shape
    tm = min(ROW_TILE, m)
    row = lambda i: (i, 0)
    const = lambda i: (0, 0)
    tok = pl.BlockSpec((tm, RW_WIDTH), row)
    vec = pl.BlockSpec((1, RW_WIDTH), const)
    return pl.pallas_call(
        _outproj_kernel,
        grid=(m // tm,),
        in_specs=[pl.BlockSpec((tm, d), row), pl.BlockSpec((tm, NA_WIDTH), row), tok, tok, tok, tok, tok,
                  vec, vec, pl.BlockSpec((MXU_DIM, MXU_DIM), const),
                  pl.BlockSpec((NA_WIDTH + RW_WIDTH, d), const)],
        out_specs=pl.BlockSpec((tm, d), row),
        out_shape=jax.ShapeDtypeStruct((m, d), F32),
        compiler_params=_params("parallel"),
        name="outproj",
    )(h, yna, y0, y1, b0, b1, g, lnx_w, lnx_b, hm, w_out)


def _pad_lora(w_up):
    zero = jnp.zeros_like(w_up[0])
    return (jnp.concatenate([w_up[0], zero], axis=0).astype(BF16),
            jnp.concatenate([zero, w_up[1]], axis=0).astype(BF16))


def _layer_weights(w, i):
    row = lambda x: x.reshape(1, -1)
    wup = _pad_lora(w["rw_w_up"][i])
    aup = _pad_lora(w["rw_a_up"][i])
    lane = jnp.arange(MXU_DIM)
    hm = (lane[:, None] // HEAD_DIM == lane[None, :] // HEAD_DIM).astype(BF16)
    common = dict(mu=row(w["rw_mu"][i]), gup=w["rw_g_up"][i].astype(BF16), k_k=row(w["rw_k_k"][i]),
                  k_a=row(w["rw_k_a"][i]), r_k=row(w["rw_r_k"][i]), hm=hm)
    dirs = [dict(common, w0=row(w["rw_w0"][i, d]), wup=wup[d], a0=row(w["rw_a0"][i, d]), aup=aup[d])
            for d in range(2)]
    return dict(
        ffn1=(row(w["ffn1_norm"][i]), w["ffn1_wg"][i].astype(BF16), w["ffn1_wu"][i].astype(BF16),
              w["ffn1_wd"][i].astype(BF16)),
        ffn2=(row(w["ffn2_norm"][i]), w["ffn2_wg"][i].astype(BF16), w["ffn2_wu"][i].astype(BF16),
              w["ffn2_wd"][i].astype(BF16)),
        mix_norm=row(w["mix_norm"][i]), w_in=w["w_in"][i].astype(BF16),
        tz=_na_bias_table(w["na_rpb"][i]), dirs=dirs, hm=hm,
        lnx_w=row(w["rw_lnx_w"][i]), lnx_b=row(w["rw_lnx_b"][i]), w_out=w["w_out"][i].astype(BF16),
        ple=(row(w["ple_norm"][i]), w["ple_gate"][i].astype(BF16), w["ple_up"][i].astype(BF16)),
    )


def _trunk(x, p, layers, final_norm):
    b, t, d = x.shape
    m = b * t
    h = x.reshape(m, d)
    depth = len(layers)
    for i, lw in enumerate(layers):
        h = _ffn(h, *lw["ffn1"])
        qkv, u = _inproj(h, lw["mix_norm"], lw["w_in"])
        yna = _na(qkv.reshape(b, t, -1), lw["tz"])
        u3 = u.reshape(b, t, -1)
        y0, b0, g = _wkv(u3, lw["dirs"][0], reverse=False)
        y1, b1, _ = _wkv(u3, lw["dirs"][1], reverse=True)
        flat = lambda z: z.reshape(m, -1)
        h = _outproj(h, flat(yna), flat(y0), flat(y1), flat(b0), flat(b1), flat(g),
                     lw["lnx_w"], lw["lnx_b"], lw["hm"], lw["w_out"])
        pn, pg, pu = lw["ple"]
        h = _ffn(h, *lw["ffn2"], ple=(p[i].reshape(m, -1), pn, pg, pu),
                 final_norm=final_norm if i == depth - 1 else None)
    return h.reshape(b, t, d)


def kernel(x_prompt, x_sample, p_prompt, p_sample, ffn1_norm, ffn1_wg, ffn1_wu, ffn1_wd, mix_norm, w_in, na_rpb, rw_mu, rw_w0, rw_w_up, rw_a0, rw_a_up, rw_g_up, rw_k_k, rw_k_a, rw_r_k, rw_lnx_w, rw_lnx_b, w_out, ffn2_norm, ffn2_wg, ffn2_wu, ffn2_wd, ple_norm, ple_gate, ple_up, final_norm):
    w = dict(ffn1_norm=ffn1_norm, ffn1_wg=ffn1_wg, ffn1_wu=ffn1_wu, ffn1_wd=ffn1_wd, mix_norm=mix_norm,
             w_in=w_in, na_rpb=na_rpb, rw_mu=rw_mu, rw_w0=rw_w0, rw_w_up=rw_w_up, rw_a0=rw_a0,
             rw_a_up=rw_a_up, rw_g_up=rw_g_up, rw_k_k=rw_k_k, rw_k_a=rw_k_a, rw_r_k=rw_r_k,
             rw_lnx_w=rw_lnx_w, rw_lnx_b=rw_lnx_b, w_out=w_out, ffn2_norm=ffn2_norm, ffn2_wg=ffn2_wg,
             ffn2_wu=ffn2_wu, ffn2_wd=ffn2_wd, ple_norm=ple_norm, ple_gate=ple_gate, ple_up=ple_up)
    layers = [_layer_weights(w, i) for i in range(ffn1_wg.shape[0])]
    fn = final_norm.reshape(1, -1)
    return (_trunk(x_prompt, p_prompt, layers, fn), _trunk(x_sample, p_sample, layers, fn))
```

```python
import functools

import jax
import jax.numpy as jnp
from jax import lax
from jax.experimental import pallas as pl
from jax.experimental.pallas import tpu as pltpu

F32 = jnp.float32
BF16 = jnp.bfloat16

GRID_W = 64
HEAD_DIM = 64
NA_HEADS = 8
RW_HEADS = 8
NA_WIDTH = NA_HEADS * HEAD_DIM
RW_WIDTH = RW_HEADS * HEAD_DIM
WIN_ROWS = 8
WIN_COLS = 16
DECAY_LORA = 64
AAA_LORA = 64
GATE_LORA = 128
RW_COLS = 3 * RW_WIDTH + 2 * DECAY_LORA + 2 * AAA_LORA + GATE_LORA
NORM_EPS = 1e-6
LNX_EPS = 64e-5
DECAY_SCALE = 0.606531
MASK_VALUE = -1e30

LANES = 128
MXU_DIM = 256
VMEM_LIMIT_BYTES = 52 * 1024 * 1024

ROW_TILE = 512
FF_TILE = 1408
NA_ROWS = 8
NA_UNROLL = 2
CHUNK = 64
WKV_STEP = 256
GROUP_HEADS = MXU_DIM // HEAD_DIM


def _params(*sem):
    return pltpu.CompilerParams(dimension_semantics=sem, vmem_limit_bytes=VMEM_LIMIT_BYTES)


def _bdot(a, b):
    return jnp.dot(a.astype(BF16), b.astype(BF16), preferred_element_type=F32)


def _bdot_nt(a, b):
    return lax.dot_general(a.astype(BF16), b.astype(BF16), (((1,), (1,)), ((), ())),
                           preferred_element_type=F32)


def _bdot_tn(a, b):
    return lax.dot_general(a.astype(BF16), b.astype(BF16), (((0,), (0,)), ((), ())),
                           preferred_element_type=F32)


def _split(x, n):
    parts = []
    rem = x
    for _ in range(n):
        p = rem.astype(BF16)
        parts.append(p)
        rem = rem - p.astype(F32)
    return parts


def _rms(x, g):
    ms = jnp.mean(x * x, axis=-1, keepdims=True)
    return x * lax.rsqrt(ms + NORM_EPS) * g


def _sigmoid(x):
    return 1.0 / (1.0 + jnp.exp(-x))


def _ffn_kernel(*refs, n_ff, with_ple, with_final):
    if with_ple:
        (x_ref, g_ref, wg_ref, wu_ref, wd_ref, p_ref, pn_ref, pg_ref, pu_ref, fn_ref,
         o_ref, xn_ref, acc_ref) = refs
    else:
        x_ref, g_ref, wg_ref, wu_ref, wd_ref, o_ref, xn_ref, acc_ref = refs
    j = pl.program_id(1)

    @pl.when(j == 0)
    def _():
        xn_ref[...] = _rms(x_ref[...], g_ref[...]).astype(BF16)
        acc_ref[...] = jnp.zeros_like(acc_ref)

    xn = xn_ref[...]
    a = jnp.dot(xn, wg_ref[...], preferred_element_type=F32)
    b = jnp.dot(xn, wu_ref[...], preferred_element_type=F32)
    mid = (a * _sigmoid(a)) * b
    acc_ref[...] += jnp.dot(mid.astype(BF16), wd_ref[...], preferred_element_type=F32)

    @pl.when(j == n_ff - 1)
    def _():
        h = x_ref[...] + 0.5 * acc_ref[...]
        if with_ple:
            gate = _sigmoid(_bdot(_rms(h, pn_ref[...]), pg_ref[...]))
            h = h + gate * _bdot(p_ref[...], pu_ref[...])
            if with_final:
                h = _rms(h, fn_ref[...])
        o_ref[...] = h


def _ffn(x, g, wg, wu, wd, ple=None, final_norm=None):
    m, d = x.shape
    dff = wg.shape[1]
    tm = min(ROW_TILE, m)
    tf = FF_TILE if dff % FF_TILE == 0 else dff
    n_ff = dff // tf
    row = lambda i, j: (i, 0)
    const = lambda i, j: (0, 0)
    in_specs = [
        pl.BlockSpec((tm, d), row),
        pl.BlockSpec((1, d), const),
        pl.BlockSpec((d, tf), lambda i, j: (0, j)),
        pl.BlockSpec((d, tf), lambda i, j: (0, j)),
        pl.BlockSpec((tf, d), lambda i, j: (j, 0)),
    ]
    args = [x, g, wg, wu, wd]
    if ple is not None:
        p, pn, pg, pu = ple
        fn = final_norm if final_norm is not None else pn
        in_specs += [
            pl.BlockSpec((tm, p.shape[1]), row),
            pl.BlockSpec((1, d), const),
            pl.BlockSpec((d, d), const),
            pl.BlockSpec((p.shape[1], d), const),
            pl.BlockSpec((1, d), const),
        ]
        args += [p, pn, pg, pu, fn]
    kern = functools.partial(_ffn_kernel, n_ff=n_ff, with_ple=ple is not None,
                             with_final=final_norm is not None)
    return pl.pallas_call(
        kern,
        grid=(m // tm, n_ff),
        in_specs=in_specs,
        out_specs=pl.BlockSpec((tm, d), row),
        out_shape=jax.ShapeDtypeStruct((m, d), F32),
        scratch_shapes=[pltpu.VMEM((tm, d), BF16), pltpu.VMEM((tm, d), F32)],
        compiler_params=_params("parallel", "arbitrary"),
        name="ffn",
    )(*args)


def _inproj_kernel(x_ref, xp_ref, xn_ref, g_ref, wq_ref, wr_ref, mu_ref, qkv_ref, um_ref, *, seq_len):
    i = pl.program_id(0)
    tm = x_ref.shape[0]
    g = g_ref[...]
    xn = _rms(x_ref[...], g).astype(BF16)
    proj = jnp.dot(xn, wq_ref[...], preferred_element_type=F32)
    lane = lax.broadcasted_iota(jnp.int32, (1, 3 * NA_WIDTH), 1)
    scale = jnp.where(lane < NA_WIDTH, HEAD_DIM ** -0.5, 1.0)
    qkv_ref[...] = (proj * scale).astype(BF16)

    ext = jnp.concatenate([_rms(xp_ref[...], g).astype(BF16), xn, _rms(xn_ref[...], g).astype(BF16)], axis=0)
    u = jnp.dot(ext, wr_ref[...], preferred_element_type=F32)
    has_prev = (i * tm) % seq_len != 0
    has_next = ((i + 1) * tm) % seq_len != 0
    u = jnp.concatenate([jnp.where(has_prev, u[0:8], 0.0), u[8:tm + 8],
                         jnp.where(has_next, u[tm + 8:], 0.0)], axis=0)
    nbr = pltpu.roll(u, 1, axis=0) + pltpu.roll(u, tm + 15, axis=0)
    uc = u[8:tm + 8]
    um_ref[...] = uc + mu_ref[...] * (0.5 * nbr[8:tm + 8] - uc)


def _inproj(x, g, w_qkv, w_rw, mu, seq_len):
    m, d = x.shape
    tm = min(ROW_TILE, m, seq_len)
    na3 = 3 * NA_WIDTH
    hb = tm // 8
    nhb = m // 8
    const = lambda i: (0, 0)
    return pl.pallas_call(
        functools.partial(_inproj_kernel, seq_len=seq_len),
        grid=(m // tm,),
        in_specs=[
            pl.BlockSpec((tm, d), lambda i: (i, 0)),
            pl.BlockSpec((8, d), lambda i: (jnp.maximum(i * hb - 1, 0), 0)),
            pl.BlockSpec((8, d), lambda i: (jnp.minimum((i + 1) * hb, nhb - 1), 0)),
            pl.BlockSpec((1, d), const),
            pl.BlockSpec((d, na3), const),
            pl.BlockSpec((d, RW_COLS), const),
            pl.BlockSpec((1, RW_COLS), const),
        ],
        out_specs=[
            pl.BlockSpec((tm, na3), lambda i: (i, 0)),
            pl.BlockSpec((tm, RW_COLS), lambda i: (i, 0)),
        ],
        out_shape=[jax.ShapeDtypeStruct((m, na3), BF16), jax.ShapeDtypeStruct((m, RW_COLS), F32)],
        compiler_params=_params("parallel"),
        name="inproj",
    )(x, x, x, g, w_qkv, w_rw, mu)


def _na_bias_table(rpb):
    j = jnp.arange(GRID_W)
    c = jnp.arange(GRID_W)
    cs = jnp.clip(j - WIN_COLS // 2, 0, GRID_W - WIN_COLS)
    valid = (c[None, :] >= cs[:, None]) & (c[None, :] < cs[:, None] + WIN_COLS)
    cidx = jnp.clip(c[None, :] - j[:, None] + (WIN_COLS - 1), 0, 2 * WIN_COLS - 2)
    full = jnp.where(valid, rpb[:, :, cidx], MASK_VALUE)
    nri = 2 * WIN_ROWS - 2
    two = jnp.concatenate([full[:, 0:nri], full[:, 1:nri + 1]], axis=-1)
    two = two.reshape(NA_HEADS // 2, 2, nri, GRID_W, 2 * GRID_W)
    return jnp.transpose(two, (0, 2, 1, 3, 4)).reshape(NA_HEADS // 2, nri, 2 * GRID_W, 2 * GRID_W)


def _na_kernel(q_ref, kp_ref, kc_ref, kn_ref, vp_ref, vc_ref, vn_ref, tz_ref, o_ref,
               kbuf, vbuf, *, rows):
    qb = pl.program_id(1)
    blk = NA_ROWS * GRID_W
    kbuf[0:blk] = kp_ref[...]
    kbuf[blk:2 * blk] = kc_ref[...]
    kbuf[2 * blk:3 * blk] = kn_ref[...]
    vbuf[0:blk] = vp_ref[...]
    vbuf[blk:2 * blk] = vc_ref[...]
    vbuf[2 * blk:3 * blk] = vn_ref[...]
    lane = lax.broadcasted_iota(jnp.int32, (GRID_W, 2 * HEAD_DIM), 1)
    low = lane < HEAD_DIM
    win = WIN_ROWS * GRID_W

    npair = NA_HEADS // 2
    sls = [slice(hp * 2 * HEAD_DIM, (hp + 1) * 2 * HEAD_DIM) for hp in range(npair)]
    zero = jnp.zeros((GRID_W, 2 * HEAD_DIM), BF16)

    def body(it, carry):
        qoff, koff, delta = [], [], []
        for n in range(NA_UNROLL):
            il = it * NA_UNROLL + n
            i = qb * NA_ROWS + il
            rs = jnp.clip(i - WIN_ROWS // 2, 0, rows - WIN_ROWS)
            delta.append(i - rs)
            koff.append(pl.multiple_of((rs - qb * NA_ROWS + NA_ROWS) * GRID_W, GRID_W))
            qoff.append(pl.multiple_of(il * GRID_W, GRID_W))
        chains = [(n, hp) for n in range(NA_UNROLL) for hp in range(npair)]
        qps = [q_ref[pl.ds(qoff[n], GRID_W), sls[hp]] for n, hp in chains]
        qsts = [jnp.concatenate([jnp.where(low, qp, zero), jnp.where(low, zero, qp)], axis=0)
                for qp in qps]
        ss = [lax.dot_general(qsts[ci], kbuf[pl.ds(koff[n], win), sls[hp]], (((1,), (1,)), ((), ())),
                              preferred_element_type=F32)
              for ci, (n, hp) in enumerate(chains)]
        ss = [ss[ci] + jnp.concatenate(
            [tz_ref[hp, 2 * p - delta[n] + (WIN_ROWS - 1)] for p in range(WIN_ROWS // 2)], axis=1)
            for ci, (n, hp) in enumerate(chains)]
        ps = [jnp.exp(s - jnp.max(s, axis=1, keepdims=True)) for s in ss]
        ps = [p * (1.0 / jnp.sum(p, axis=1, keepdims=True)) for p in ps]
        os_ = [jnp.dot(ps[ci].astype(BF16), vbuf[pl.ds(koff[n], win), sls[hp]],
                       preferred_element_type=F32) for ci, (n, hp) in enumerate(chains)]
        outs = [jnp.where(low, o[0:GRID_W], o[GRID_W:2 * GRID_W]) for o in os_]
        for n in range(NA_UNROLL):
            o_ref[pl.ds(qoff[n], GRID_W), :] = jnp.concatenate(
                outs[n * npair:(n + 1) * npair], axis=1).astype(BF16)
        return carry

    lax.fori_loop(0, NA_ROWS // NA_UNROLL, body, 0)


def _na(qkv, tz):
    b, t, _ = qkv.shape
    rows = t // GRID_W
    nqb = rows // NA_ROWS
    blk = NA_ROWS * GRID_W
    spec = lambda col, shift: pl.BlockSpec(
        (None, blk, NA_WIDTH),
        lambda bi, qi: (bi, jnp.clip(qi + shift, 0, nqb - 1), col))
    return pl.pallas_call(
        functools.partial(_na_kernel, rows=rows),
        grid=(b, nqb),
        in_specs=[spec(0, 0), spec(1, -1), spec(1, 0), spec(1, 1), spec(2, -1), spec(2, 0), spec(2, 1),
                  pl.BlockSpec(tz.shape, lambda bi, qi: (0, 0, 0, 0))],
        out_specs=pl.BlockSpec((None, blk, NA_WIDTH), lambda bi, qi: (bi, qi, 0)),
        out_shape=jax.ShapeDtypeStruct((b, t, NA_WIDTH), BF16),
        scratch_shapes=[pltpu.VMEM((3 * blk, NA_WIDTH), BF16), pltpu.VMEM((3 * blk, NA_WIDTH), BF16)],
        compiler_params=_params("parallel", "arbitrary"),
        name="natten",
    )(qkv, qkv, qkv, qkv, qkv, qkv, qkv, tz)


def _head_sum(x, hm):
    outs = []
    for q in range(RW_WIDTH // MXU_DIM):
        xs = x[:, q * MXU_DIM:(q + 1) * MXU_DIM]
        outs.append(sum(jnp.dot(p, hm, preferred_element_type=F32) for p in _split(xs, 2)))
    return jnp.concatenate(outs, axis=1)


def _stack(x, low):
    xb = x.astype(BF16)
    zero = jnp.zeros((x.shape[0], LANES), BF16)
    blocks = []
    for h in range(GROUP_HEADS):
        tile = xb[:, (h // 2) * LANES:(h // 2 + 1) * LANES]
        kept = jnp.where(low, tile, zero) if h % 2 == 0 else jnp.where(low, zero, tile)
        blocks.append(jnp.concatenate([kept, zero] if h < 2 else [zero, kept], axis=1))
    return jnp.concatenate(blocks, axis=0)


def _head_transpose(x):
    xt = jnp.concatenate([x, jnp.zeros_like(x)], axis=0).T
    b = [xt[h * HEAD_DIM:(h + 1) * HEAD_DIM] for h in range(GROUP_HEADS)]
    return jnp.concatenate([b[0] + pltpu.roll(b[1], HEAD_DIM, axis=1),
                            b[2] + pltpu.roll(b[3], HEAD_DIM, axis=1)], axis=1)


def _wkv_kernel(r_ref, k_ref, v_ref, wl_ref, al_ref, w0_ref, wup_ref, a0_ref, aup_ref,
                kk_ref, ka_ref, hm_ref, y_ref, z_ref, *, reverse):
    ts = r_ref.shape[0]
    nchunk = ts // CHUNK
    ngroup = RW_WIDTH // MXU_DIM

    @pl.when(pl.program_id(1) == 0)
    def _():
        z_ref[...] = jnp.zeros_like(z_ref)

    r = r_ref[...]
    k = k_ref[...]
    v = v_ref[...]
    logw = -DECAY_SCALE * _sigmoid(w0_ref[...] + _bdot(jnp.tanh(wl_ref[...]), wup_ref[...]))
    a = _sigmoid(a0_ref[...] + _bdot(al_ref[...], aup_ref[...]))
    kk = k * kk_ref[...]
    kk = kk / jnp.maximum(jnp.sqrt(_head_sum(kk * kk, hm_ref[...])), 1e-12)
    kd = k * (1.0 + (a - 1.0) * ka_ref[...])
    bb = kk * a

    ti = lax.broadcasted_iota(jnp.int32, (CHUNK, CHUNK), 0)
    si = lax.broadcasted_iota(jnp.int32, (CHUNK, CHUNK), 1)
    tri = jnp.where((si >= ti) if reverse else (si <= ti), 1.0, 0.0).astype(BF16)
    lane = lax.broadcasted_iota(jnp.int32, (CHUNK, MXU_DIM), 1)
    trow = lax.broadcasted_iota(jnp.int32, (CHUNK, MXU_DIM), 0)
    s_of_lane = lane % HEAD_DIM
    if reverse:
        strict, incl = s_of_lane > trow, s_of_lane >= trow
    else:
        strict, incl = s_of_lane < trow, s_of_lane <= trow
    eye_lc = jnp.where(s_of_lane == trow, 1.0, 0.0)
    low = lax.broadcasted_iota(jnp.int32, (CHUNK, LANES), 1) < HEAD_DIM

    chains = [(c, q) for c in range(nchunk) for q in range(ngroup)]
    lanes = [slice(q * MXU_DIM, (q + 1) * MXU_DIM) for q in range(ngroup)]
    rows = [slice(c * CHUNK, (c + 1) * CHUNK) for c in range(nchunk)]
    st = lambda x: _stack(x, low)

    cums = [sum(jnp.dot(tri, p, preferred_element_type=F32) for p in _split(logw[rows[c]], 3))
            for c in range(nchunk)]
    dec = []
    for c in range(nchunk):
        cum, lw = cums[c], logw[rows[c]]
        cmid = cum[CHUNK // 2:CHUNK // 2 + 1]
        ctot = cum[0:1] if reverse else cum[CHUNK - 1:CHUNK]
        e1 = jnp.exp(cum - cmid)
        e1i = jnp.exp(cmid - cum)
        ee = jnp.exp(ctot - cum)
        emid = jnp.exp(cmid)
        r_g = r[rows[c]] * e1
        a_g = -kk[rows[c]] * (e1 * jnp.exp(-lw))
        dec.append(dict(
            r_g=r_g, a_g=a_g, b_g=bb[rows[c]] * e1i, k_g=kd[rows[c]] * e1i, r_s=r_g * emid,
            a_s=a_g * emid, b_e=bb[rows[c]] * ee, k_e=kd[rows[c]] * ee, v=v[rows[c]],
            wtot=jnp.exp(ctot)))

    st_v = [st(dec[c]["v"][:, lanes[q]]) for c, q in chains]
    grams = [_bdot_nt(jnp.concatenate([dec[c]["a_g"][:, lanes[q]], dec[c]["r_g"][:, lanes[q]]], axis=0),
                      jnp.concatenate([st(dec[c]["b_g"][:, lanes[q]]), st(dec[c]["k_g"][:, lanes[q]])],
                                      axis=0)) for c, q in chains]
    a_ab = [jnp.where(strict, g[0:CHUNK, 0:MXU_DIM], 0.0) for g in grams]
    a_kr = [jnp.concatenate([jnp.where(strict, g[0:CHUNK, MXU_DIM:], 0.0),
                             jnp.where(incl, g[CHUNK:, MXU_DIM:], 0.0)], axis=0) for g in grams]
    a_rb = [jnp.where(incl, g[CHUNK:, 0:MXU_DIM], 0.0) for g in grams]
    tinv = [eye_lc + a for a in a_ab]
    apow = [_bdot(a, st(a)) for a in a_ab]
    npow = CHUNK.bit_length() - 2
    for jj in range(npow):
        if jj < npow - 1:
            both = [_bdot(jnp.concatenate([ap, ti], axis=0), st(ap)) for ap, ti in zip(apow, tinv)]
            apow = [b[0:CHUNK] for b in both]
            tinv = [ti + b[CHUNK:] for ti, b in zip(tinv, both)]
        else:
            tinv = [ti + _bdot(ti, st(ap)) for ap, ti in zip(apow, tinv)]
    xy = [_bdot(a_kr[i], st_v[i]) for i in range(len(chains))]
    au = [_bdot(tinv[i], jnp.concatenate([st(dec[c]["a_s"][:, lanes[q]]), st(xy[i][0:CHUNK])], axis=1))
          for i, (c, q) in enumerate(chains)]
    st_au = [jnp.concatenate([st(x[:, 0:MXU_DIM]), st(x[:, MXU_DIM:])], axis=1) for x in au]
    ry = [_bdot(a_rb[i], st_au[i]) for i in range(len(chains))]
    pq = [_bdot(_head_transpose(dec[c]["b_e"][:, lanes[q]]), st_au[i]) for i, (c, q) in enumerate(chains)]
    q2 = [_bdot(_head_transpose(dec[c]["k_e"][:, lanes[q]]), st_v[i]) for i, (c, q) in enumerate(chains)]
    chunk_terms = [[None] * ngroup for _ in range(nchunk)]
    for i, (c, q) in enumerate(chains):
        r_hat = dec[c]["r_s"][:, lanes[q]] + ry[i][:, 0:MXU_DIM]
        y_hat = ry[i][:, MXU_DIM:] + xy[i][CHUNK:]
        p_lc = eye_lc * dec[c]["wtot"][:, lanes[q]] + pq[i][:, 0:MXU_DIM]
        q_lc = pq[i][:, MXU_DIM:] + q2[i]
        chunk_terms[c][q] = (jnp.concatenate([r_hat, p_lc], axis=0), y_hat, q_lc)

    order = range(nchunk - 1, -1, -1) if reverse else range(nchunk)
    zs = [z_ref[q] for q in range(ngroup)]
    for c in order:
        ys = []
        for q in range(ngroup):
            rp, y_hat, q_lc = chunk_terms[c][q]
            both = _bdot(rp, st(zs[q]))
            ys.append(both[0:CHUNK] + y_hat)
            zs[q] = both[CHUNK:] + q_lc
        y_ref[c * CHUNK:(c + 1) * CHUNK, :] = jnp.concatenate(ys, axis=1)
    for q in range(ngroup):
        z_ref[q] = zs[q]


_UM_WL_BLOCK = 3 * RW_WIDTH // LANES
_UM_AL_BLOCK = _UM_WL_BLOCK + 1
_UM_GL_BLOCK = _UM_WL_BLOCK + 2


def _wkv(um, wp, reverse):
    b, t, _ = um.shape
    ts = min(WKV_STEP, t)
    nsteps = t // ts
    pos = (lambda si: nsteps - 1 - si) if reverse else (lambda si: si)
    const2 = lambda bi, si: (0, 0)
    vec = pl.BlockSpec((1, RW_WIDTH), const2)
    wide = lambda col: pl.BlockSpec((None, ts, RW_WIDTH), lambda bi, si: (bi, pos(si), col))
    narrow = lambda col: pl.BlockSpec((None, ts, LANES), lambda bi, si: (bi, pos(si), col))
    return pl.pallas_call(
        functools.partial(_wkv_kernel, reverse=reverse),
        grid=(b, nsteps),
        in_specs=[
            wide(0), wide(1), wide(2), narrow(_UM_WL_BLOCK), narrow(_UM_AL_BLOCK),
            vec,
            pl.BlockSpec((2 * DECAY_LORA, RW_WIDTH), const2),
            vec,
            pl.BlockSpec((2 * AAA_LORA, RW_WIDTH), const2),
            vec, vec,
            pl.BlockSpec((MXU_DIM, MXU_DIM), const2),
        ],
        out_specs=wide(0),
        out_shape=jax.ShapeDtypeStruct((b, t, RW_WIDTH), F32),
        scratch_shapes=[pltpu.VMEM((RW_WIDTH // MXU_DIM, HEAD_DIM, MXU_DIM), F32)],
        compiler_params=_params("parallel", "arbitrary"),
        name="wkv_bwd" if reverse else "wkv_fwd",
    )(um, um, um, um, um, wp["w0"], wp["wup"], wp["a0"], wp["aup"], wp["k_k"], wp["k_a"], wp["hm"])


def _outproj_kernel(h_ref, yna_ref, y0_ref, y1_ref, r_ref, k_ref, v_ref, al_ref, gl_ref,
                    a00_ref, aup0_ref, a01_ref, aup1_ref, ka_ref, rk_ref, gup_ref, lw_ref, lb_ref,
                    hm_ref, w_ref, o_ref):
    hm = hm_ref[...]
    wkv = y0_ref[...] + y1_ref[...]
    mean = _head_sum(wkv, hm) * (1.0 / HEAD_DIM)
    dev = wkv - mean
    var = _head_sum(dev * dev, hm) * (1.0 / HEAD_DIM)
    yn = dev * lax.rsqrt(var + LNX_EPS) * lw_ref[...] + lb_ref[...]
    al = al_ref[...]
    a_sum = (_sigmoid(a00_ref[...] + _bdot(al, aup0_ref[...]))
             + _sigmoid(a01_ref[...] + _bdot(al, aup1_ref[...])))
    kd_sum = k_ref[...] * (2.0 + (a_sum - 2.0) * ka_ref[...])
    bonus = _head_sum(r_ref[...] * kd_sum * rk_ref[...], hm) * v_ref[...]
    gate = _bdot(_sigmoid(gl_ref[...]), gup_ref[...])
    y_rw = (yn + bonus) * gate
    mix = jnp.concatenate([yna_ref[...], y_rw.astype(BF16)], axis=1)
    o_ref[...] = h_ref[...] + jnp.dot(mix, w_ref[...], preferred_element_type=F32)


def _outproj(h, yna, y0, y1, um, lw):
    m, d = h.shape
    tm = min(ROW_TILE, m)
    row = lambda i: (i, 0)
    const = lambda i: (0, 0)
    tok = pl.BlockSpec((tm, RW_WIDTH), row)
    vec = pl.BlockSpec((1, RW_WIDTH), const)
    um_wide = lambda col: pl.BlockSpec((tm, RW_WIDTH), lambda i: (i, col))
    um_narrow = lambda col: pl.BlockSpec((tm, LANES), lambda i: (i, col))
    lora = pl.BlockSpec((2 * AAA_LORA, RW_WIDTH), const)
    d0, d1 = lw["dirs"]
    return pl.pallas_call(
        _outproj_kernel,
        grid=(m // tm,),
        in_specs=[pl.BlockSpec((tm, d), row), pl.BlockSpec((tm, NA_WIDTH), row), tok, tok,
                  um_wide(0), um_wide(1), um_wide(2), um_narrow(_UM_AL_BLOCK), um_narrow(_UM_GL_BLOCK),
                  vec, lora, vec, lora, vec, vec, pl.BlockSpec((GATE_LORA, RW_WIDTH), const), vec, vec,
                  pl.BlockSpec((MXU_DIM, MXU_DIM), const),
                  pl.BlockSpec((NA_WIDTH + RW_WIDTH, d), const)],
        out_specs=pl.BlockSpec((tm, d), row),
        out_shape=jax.ShapeDtypeStruct((m, d), F32),
        compiler_params=_params("parallel"),
        name="outproj",
    )(h, yna, y0, y1, um, um, um, um, um, d0["a0"], d0["aup"], d1["a0"], d1["aup"], d0["k_a"],
      lw["r_k"], lw["gup"], lw["lnx_w"], lw["lnx_b"], lw["hm"], lw["w_out"])


def _pad_lora(w_up):
    zero = jnp.zeros_like(w_up[0])
    return (jnp.concatenate([w_up[0], zero], axis=0).astype(BF16),
            jnp.concatenate([zero, w_up[1]], axis=0).astype(BF16))


def _layer_weights(w, i):
    row = lambda x: x.reshape(1, -1)
    wup = _pad_lora(w["rw_w_up"][i])
    aup = _pad_lora(w["rw_a_up"][i])
    lane = jnp.arange(MXU_DIM)
    hm = (lane[:, None] // HEAD_DIM == lane[None, :] // HEAD_DIM).astype(BF16)
    common = dict(k_k=row(w["rw_k_k"][i]), k_a=row(w["rw_k_a"][i]), hm=hm)
    dirs = [dict(common, w0=row(w["rw_w0"][i, d]), wup=wup[d], a0=row(w["rw_a0"][i, d]), aup=aup[d])
            for d in range(2)]
    w_in = w["w_in"][i].astype(BF16)
    return dict(
        ffn1=(row(w["ffn1_norm"][i]), w["ffn1_wg"][i].astype(BF16), w["ffn1_wu"][i].astype(BF16),
              w["ffn1_wd"][i].astype(BF16)),
        ffn2=(row(w["ffn2_norm"][i]), w["ffn2_wg"][i].astype(BF16), w["ffn2_wu"][i].astype(BF16),
              w["ffn2_wd"][i].astype(BF16)),
        mix_norm=row(w["mix_norm"][i]), w_qkv=w_in[:, :3 * NA_WIDTH], w_rw=w_in[:, 3 * NA_WIDTH:],
        mu=row(w["rw_mu"][i]), tz=_na_bias_table(w["na_rpb"][i]), dirs=dirs, hm=hm,
        r_k=row(w["rw_r_k"][i]), gup=w["rw_g_up"][i].astype(BF16),
        lnx_w=row(w["rw_lnx_w"][i]), lnx_b=row(w["rw_lnx_b"][i]), w_out=w["w_out"][i].astype(BF16),
        ple=(row(w["ple_norm"][i]), w["ple_gate"][i].astype(BF16), w["ple_up"][i].astype(BF16)),
    )


def _trunk(x, p, layers, final_norm):
    b, t, d = x.shape
    m = b * t
    h = x.reshape(m, d)
    depth = len(layers)
    for i, lw in enumerate(layers):
        h = _ffn(h, *lw["ffn1"])
        qkv, um = _inproj(h, lw["mix_norm"], lw["w_qkv"], lw["w_rw"], lw["mu"], seq_len=t)
        yna = _na(qkv.reshape(b, t, -1), lw["tz"])
        um3 = um.reshape(b, t, -1)
        y0 = _wkv(um3, lw["dirs"][0], reverse=False)
        y1 = _wkv(um3, lw["dirs"][1], reverse=True)
        h = _outproj(h, yna.reshape(m, -1), y0.reshape(m, -1), y1.reshape(m, -1), um, lw)
        pn, pg, pu = lw["ple"]
        h = _ffn(h, *lw["ffn2"], ple=(p[i].reshape(m, -1), pn, pg, pu),
                 final_norm=final_norm if i == depth - 1 else None)
    return h.reshape(b, t, d)


def kernel(x_prompt, x_sample, p_prompt, p_sample, ffn1_norm, ffn1_wg, ffn1_wu, ffn1_wd, mix_norm, w_in, na_rpb, rw_mu, rw_w0, rw_w_up, rw_a0, rw_a_up, rw_g_up, rw_k_k, rw_k_a, rw_r_k, rw_lnx_w, rw_lnx_b, w_out, ffn2_norm, ffn2_wg, ffn2_wu, ffn2_wd, ple_norm, ple_gate, ple_up, final_norm):
    w = dict(ffn1_norm=ffn1_norm, ffn1_wg=ffn1_wg, ffn1_wu=ffn1_wu, ffn1_wd=ffn1_wd, mix_norm=mix_norm,
             w_in=w_in, na_rpb=na_rpb, rw_mu=rw_mu, rw_w0=rw_w0, rw_w_up=rw_w_up, rw_a0=rw_a0,
             rw_a_up=rw_a_up, rw_g_up=rw_g_up, rw_k_k=rw_k_k, rw_k_a=rw_k_a, rw_r_k=rw_r_k,
             rw_lnx_w=rw_lnx_w, rw_lnx_b=rw_lnx_b, w_out=w_out, ffn2_norm=ffn2_norm, ffn2_wg=ffn2_wg,
             ffn2_wu=ffn2_wu, ffn2_wd=ffn2_wd, ple_norm=ple_norm, ple_gate=ple_gate, ple_up=ple_up)
    layers = [_layer_weights(w, i) for i in range(ffn1_wg.shape[0])]
    fn = final_norm.reshape(1, -1)
    return (_trunk(x_prompt, p_prompt, layers, fn), _trunk(x_sample, p_sample, layers, fn))
```

```python
import functools

import jax
import jax.numpy as jnp
from jax import lax
from jax.experimental import pallas as pl
from jax.experimental.pallas import tpu as pltpu

F32 = jnp.float32
BF16 = jnp.bfloat16

GRID_W = 64
HEAD_DIM = 64
NA_HEADS = 8
RW_HEADS = 8
NA_WIDTH = NA_HEADS * HEAD_DIM
RW_WIDTH = RW_HEADS * HEAD_DIM
WIN_ROWS = 8
WIN_COLS = 16
DECAY_LORA = 64
AAA_LORA = 64
GATE_LORA = 128
RW_COLS = 3 * RW_WIDTH + 2 * DECAY_LORA + 2 * AAA_LORA + GATE_LORA
NORM_EPS = 1e-6
LNX_EPS = 64e-5
DECAY_SCALE = 0.606531
MASK_VALUE = -1e30

LANES = 128
MXU_DIM = 256
VMEM_LIMIT_BYTES = 52 * 1024 * 1024

ROW_TILE = 512
FFN_SUBTILES = 2
NA_ROWS = 8
NA_UNROLL = 2
CHUNK = 64
WKV_STEP = 256
GROUP_HEADS = MXU_DIM // HEAD_DIM


def _params(*sem):
    return pltpu.CompilerParams(dimension_semantics=sem, vmem_limit_bytes=VMEM_LIMIT_BYTES)


def _bdot(a, b):
    return jnp.dot(a.astype(BF16), b.astype(BF16), preferred_element_type=F32)


def _bdot_nt(a, b):
    return lax.dot_general(a.astype(BF16), b.astype(BF16), (((1,), (1,)), ((), ())),
                           preferred_element_type=F32)


def _split(x, n):
    parts = []
    rem = x
    for _ in range(n):
        p = rem.astype(BF16)
        parts.append(p)
        rem = rem - p.astype(F32)
    return parts


def _rms(x, g):
    ms = jnp.mean(x * x, axis=-1, keepdims=True)
    return x * lax.rsqrt(ms + NORM_EPS) * g


def _sigmoid(x):
    return 1.0 / (1.0 + jnp.exp(-x))


def _ffn_kernel(*refs, with_ple, with_final):
    if with_ple:
        x_ref, g_ref, wg_ref, wu_ref, wd_ref, p_ref, pn_ref, pg_ref, pu_ref, fn_ref, o_ref = refs
    else:
        x_ref, g_ref, wg_ref, wu_ref, wd_ref, o_ref = refs
    sub = x_ref.shape[0] // FFN_SUBTILES
    rows = [slice(i * sub, (i + 1) * sub) for i in range(FFN_SUBTILES)]
    xns = [_rms(x_ref[r], g_ref[...]).astype(BF16) for r in rows]
    gates = [jnp.dot(xn, wg_ref[...], preferred_element_type=F32) for xn in xns]
    ups = [jnp.dot(xn, wu_ref[...], preferred_element_type=F32) for xn in xns]
    mids = [((a * _sigmoid(a)) * b).astype(BF16) for a, b in zip(gates, ups)]
    downs = [jnp.dot(m, wd_ref[...], preferred_element_type=F32) for m in mids]
    hs = [x_ref[r] + 0.5 * d for r, d in zip(rows, downs)]
    if with_ple:
        pgate = [_sigmoid(_bdot(_rms(h, pn_ref[...]), pg_ref[...])) for h in hs]
        pup = [_bdot(p_ref[r], pu_ref[...]) for r in rows]
        hs = [h + gt * up for h, gt, up in zip(hs, pgate, pup)]
        if with_final:
            hs = [_rms(h, fn_ref[...]) for h in hs]
    for r, h in zip(rows, hs):
        o_ref[r] = h


def _ffn(x, g, wg, wu, wd, ple=None, final_norm=None):
    m, d = x.shape
    dff = wg.shape[1]
    tm = min(ROW_TILE, m)
    row = lambda i: (i, 0)
    const = lambda i: (0, 0)
    resident = lambda shape: pl.BlockSpec(shape, const, pipeline_mode=pl.Buffered(1))
    in_specs = [pl.BlockSpec((tm, d), row), pl.BlockSpec((1, d), const),
                resident((d, dff)), resident((d, dff)), resident((dff, d))]
    args = [x, g, wg, wu, wd]
    if ple is not None:
        p, pn, pg, pu = ple
        fn = final_norm if final_norm is not None else pn
        in_specs += [pl.BlockSpec((tm, p.shape[1]), row), pl.BlockSpec((1, d), const),
                     resident((d, d)), resident((p.shape[1], d)), pl.BlockSpec((1, d), const)]
        args += [p, pn, pg, pu, fn]
    kern = functools.partial(_ffn_kernel, with_ple=ple is not None, with_final=final_norm is not None)
    return pl.pallas_call(
        kern,
        grid=(m // tm,),
        in_specs=in_specs,
        out_specs=pl.BlockSpec((tm, d), row),
        out_shape=jax.ShapeDtypeStruct((m, d), F32),
        compiler_params=_params("parallel"),
        name="ffn",
    )(*args)


def _inproj_kernel(x_ref, xp_ref, xn_ref, g_ref, wq_ref, wr_ref, mu_ref, qkv_ref, um_ref, *, seq_len):
    i = pl.program_id(0)
    tm = x_ref.shape[0]
    g = g_ref[...]
    xn = _rms(x_ref[...], g).astype(BF16)
    proj = jnp.dot(xn, wq_ref[...], preferred_element_type=F32)
    lane = lax.broadcasted_iota(jnp.int32, (1, 3 * NA_WIDTH), 1)
    scale = jnp.where(lane < NA_WIDTH, HEAD_DIM ** -0.5, 1.0)
    qkv_ref[...] = (proj * scale).astype(BF16)

    ext = jnp.concatenate([_rms(xp_ref[...], g).astype(BF16), xn, _rms(xn_ref[...], g).astype(BF16)], axis=0)
    u = jnp.dot(ext, wr_ref[...], preferred_element_type=F32)
    has_prev = (i * tm) % seq_len != 0
    has_next = ((i + 1) * tm) % seq_len != 0
    u = jnp.concatenate([jnp.where(has_prev, u[0:8], 0.0), u[8:tm + 8],
                         jnp.where(has_next, u[tm + 8:], 0.0)], axis=0)
    nbr = pltpu.roll(u, 1, axis=0) + pltpu.roll(u, tm + 15, axis=0)
    uc = u[8:tm + 8]
    um_ref[...] = uc + mu_ref[...] * (0.5 * nbr[8:tm + 8] - uc)


def _inproj(x, g, w_qkv, w_rw, mu, seq_len):
    m, d = x.shape
    tm = min(ROW_TILE, m, seq_len)
    na3 = 3 * NA_WIDTH
    hb = tm // 8
    nhb = m // 8
    const = lambda i: (0, 0)
    return pl.pallas_call(
        functools.partial(_inproj_kernel, seq_len=seq_len),
        grid=(m // tm,),
        in_specs=[
            pl.BlockSpec((tm, d), lambda i: (i, 0)),
            pl.BlockSpec((8, d), lambda i: (jnp.maximum(i * hb - 1, 0), 0)),
            pl.BlockSpec((8, d), lambda i: (jnp.minimum((i + 1) * hb, nhb - 1), 0)),
            pl.BlockSpec((1, d), const),
            pl.BlockSpec((d, na3), const),
            pl.BlockSpec((d, RW_COLS), const),
            pl.BlockSpec((1, RW_COLS), const),
        ],
        out_specs=[
            pl.BlockSpec((tm, na3), lambda i: (i, 0)),
            pl.BlockSpec((tm, RW_COLS), lambda i: (i, 0)),
        ],
        out_shape=[jax.ShapeDtypeStruct((m, na3), BF16), jax.ShapeDtypeStruct((m, RW_COLS), F32)],
        compiler_params=_params("parallel"),
        name="inproj",
    )(x, x, x, g, w_qkv, w_rw, mu)


def _na_bias_table(rpb):
    j = jnp.arange(GRID_W)
    c = jnp.arange(GRID_W)
    cs = jnp.clip(j - WIN_COLS // 2, 0, GRID_W - WIN_COLS)
    valid = (c[None, :] >= cs[:, None]) & (c[None, :] < cs[:, None] + WIN_COLS)
    cidx = jnp.clip(c[None, :] - j[:, None] + (WIN_COLS - 1), 0, 2 * WIN_COLS - 2)
    full = jnp.where(valid, rpb[:, :, cidx], MASK_VALUE)
    nri = 2 * WIN_ROWS - 2
    two = jnp.concatenate([full[:, 0:nri], full[:, 1:nri + 1]], axis=-1)
    two = two.reshape(NA_HEADS // 2, 2, nri, GRID_W, 2 * GRID_W)
    return jnp.transpose(two, (0, 2, 1, 3, 4)).reshape(NA_HEADS // 2, nri, 2 * GRID_W, 2 * GRID_W)


def _na_kernel(q_ref, kp_ref, kc_ref, kn_ref, vp_ref, vc_ref, vn_ref, tz_ref, o_ref,
               kbuf, vbuf, *, rows):
    qb = pl.program_id(1)
    blk = NA_ROWS * GRID_W
    kbuf[0:blk] = kp_ref[...]
    kbuf[blk:2 * blk] = kc_ref[...]
    kbuf[2 * blk:3 * blk] = kn_ref[...]
    vbuf[0:blk] = vp_ref[...]
    vbuf[blk:2 * blk] = vc_ref[...]
    vbuf[2 * blk:3 * blk] = vn_ref[...]
    lane = lax.broadcasted_iota(jnp.int32, (GRID_W, 2 * HEAD_DIM), 1)
    low = lane < HEAD_DIM
    win = WIN_ROWS * GRID_W

    npair = NA_HEADS // 2
    sls = [slice(hp * 2 * HEAD_DIM, (hp + 1) * 2 * HEAD_DIM) for hp in range(npair)]
    zero = jnp.zeros((GRID_W, 2 * HEAD_DIM), BF16)

    def body(it, carry):
        qoff, koff, delta = [], [], []
        for n in range(NA_UNROLL):
            il = it * NA_UNROLL + n
            i = qb * NA_ROWS + il
            rs = jnp.clip(i - WIN_ROWS // 2, 0, rows - WIN_ROWS)
            delta.append(i - rs)
            koff.append(pl.multiple_of((rs - qb * NA_ROWS + NA_ROWS) * GRID_W, GRID_W))
            qoff.append(pl.multiple_of(il * GRID_W, GRID_W))
        chains = [(n, hp) for n in range(NA_UNROLL) for hp in range(npair)]
        qps = [q_ref[pl.ds(qoff[n], GRID_W), sls[hp]] for n, hp in chains]
        qsts = [jnp.concatenate([jnp.where(low, qp, zero), jnp.where(low, zero, qp)], axis=0)
                for qp in qps]
        ss = [lax.dot_general(qsts[ci], kbuf[pl.ds(koff[n], win), sls[hp]], (((1,), (1,)), ((), ())),
                              preferred_element_type=F32)
              for ci, (n, hp) in enumerate(chains)]
        ss = [ss[ci] + jnp.concatenate(
            [tz_ref[hp, 2 * p - delta[n] + (WIN_ROWS - 1)] for p in range(WIN_ROWS // 2)], axis=1)
            for ci, (n, hp) in enumerate(chains)]
        ps = [jnp.exp(s - jnp.max(s, axis=1, keepdims=True)) for s in ss]
        ps = [p * (1.0 / jnp.sum(p, axis=1, keepdims=True)) for p in ps]
        os_ = [jnp.dot(ps[ci].astype(BF16), vbuf[pl.ds(koff[n], win), sls[hp]],
                       preferred_element_type=F32) for ci, (n, hp) in enumerate(chains)]
        outs = [jnp.where(low, o[0:GRID_W], o[GRID_W:2 * GRID_W]) for o in os_]
        for n in range(NA_UNROLL):
            o_ref[pl.ds(qoff[n], GRID_W), :] = jnp.concatenate(
                outs[n * npair:(n + 1) * npair], axis=1).astype(BF16)
        return carry

    lax.fori_loop(0, NA_ROWS // NA_UNROLL, body, 0)


def _na(qkv, tz):
    b, t, _ = qkv.shape
    rows = t // GRID_W
    nqb = rows // NA_ROWS
    blk = NA_ROWS * GRID_W
    spec = lambda col, shift: pl.BlockSpec(
        (None, blk, NA_WIDTH),
        lambda bi, qi: (bi, jnp.clip(qi + shift, 0, nqb - 1), col))
    return pl.pallas_call(
        functools.partial(_na_kernel, rows=rows),
        grid=(b, nqb),
        in_specs=[spec(0, 0), spec(1, -1), spec(1, 0), spec(1, 1), spec(2, -1), spec(2, 0), spec(2, 1),
                  pl.BlockSpec(tz.shape, lambda bi, qi: (0, 0, 0, 0))],
        out_specs=pl.BlockSpec((None, blk, NA_WIDTH), lambda bi, qi: (bi, qi, 0)),
        out_shape=jax.ShapeDtypeStruct((b, t, NA_WIDTH), BF16),
        scratch_shapes=[pltpu.VMEM((3 * blk, NA_WIDTH), BF16), pltpu.VMEM((3 * blk, NA_WIDTH), BF16)],
        compiler_params=_params("parallel", "arbitrary"),
        name="natten",
    )(qkv, qkv, qkv, qkv, qkv, qkv, qkv, tz)


def _head_sum(x, hm):
    outs = []
    for q in range(RW_WIDTH // MXU_DIM):
        xs = x[:, q * MXU_DIM:(q + 1) * MXU_DIM]
        outs.append(sum(jnp.dot(p, hm, preferred_element_type=F32) for p in _split(xs, 2)))
    return jnp.concatenate(outs, axis=1)


def _stack(x, low):
    xb = x.astype(BF16)
    zero = jnp.zeros((x.shape[0], LANES), BF16)
    blocks = []
    for h in range(GROUP_HEADS):
        tile = xb[:, (h // 2) * LANES:(h // 2 + 1) * LANES]
        kept = jnp.where(low, tile, zero) if h % 2 == 0 else jnp.where(low, zero, tile)
        blocks.append(jnp.concatenate([kept, zero] if h < 2 else [zero, kept], axis=1))
    return jnp.concatenate(blocks, axis=0)


def _head_transpose(x):
    xt = jnp.concatenate([x, jnp.zeros_like(x)], axis=0).T
    b = [xt[h * HEAD_DIM:(h + 1) * HEAD_DIM] for h in range(GROUP_HEADS)]
    return jnp.concatenate([b[0] + pltpu.roll(b[1], HEAD_DIM, axis=1),
                            b[2] + pltpu.roll(b[3], HEAD_DIM, axis=1)], axis=1)


def _wkv_kernel(r_ref, k_ref, v_ref, wl_ref, al_ref, w0_ref, wup_ref, a0_ref, aup_ref,
                kk_ref, ka_ref, hm_ref, y_ref, z_ref, rp_ref, yh_ref, ql_ref, *, reverse, nsteps):
    g = pl.program_id(0)
    ts = r_ref.shape[0]
    nchunk = ts // CHUNK
    ngroup = RW_WIDTH // MXU_DIM

    @pl.when(g == 0)
    def _():
        z_ref[...] = jnp.zeros_like(z_ref)
        rp_ref[...] = jnp.zeros_like(rp_ref)
        yh_ref[...] = jnp.zeros_like(yh_ref)
        ql_ref[...] = jnp.zeros_like(ql_ref)

    r = r_ref[...]
    k = k_ref[...]
    v = v_ref[...]
    logw = -DECAY_SCALE * _sigmoid(w0_ref[...] + _bdot(jnp.tanh(wl_ref[...]), wup_ref[...]))
    a = _sigmoid(a0_ref[...] + _bdot(al_ref[...], aup_ref[...]))
    kk = k * kk_ref[...]
    kk = kk / jnp.maximum(jnp.sqrt(_head_sum(kk * kk, hm_ref[...])), 1e-12)
    kd = k * (1.0 + (a - 1.0) * ka_ref[...])
    bb = kk * a

    ti = lax.broadcasted_iota(jnp.int32, (CHUNK, CHUNK), 0)
    si = lax.broadcasted_iota(jnp.int32, (CHUNK, CHUNK), 1)
    tri = jnp.where((si >= ti) if reverse else (si <= ti), 1.0, 0.0).astype(BF16)
    lane = lax.broadcasted_iota(jnp.int32, (CHUNK, MXU_DIM), 1)
    trow = lax.broadcasted_iota(jnp.int32, (CHUNK, MXU_DIM), 0)
    s_of_lane = lane % HEAD_DIM
    if reverse:
        strict, incl = s_of_lane > trow, s_of_lane >= trow
    else:
        strict, incl = s_of_lane < trow, s_of_lane <= trow
    eye_lc = jnp.where(s_of_lane == trow, 1.0, 0.0)
    low = lax.broadcasted_iota(jnp.int32, (CHUNK, LANES), 1) < HEAD_DIM

    chains = [(c, q) for c in range(nchunk) for q in range(ngroup)]
    lanes = [slice(q * MXU_DIM, (q + 1) * MXU_DIM) for q in range(ngroup)]
    rows = [slice(c * CHUNK, (c + 1) * CHUNK) for c in range(nchunk)]
    st = lambda x: _stack(x, low)

    order = list(range(nchunk - 1, -1, -1) if reverse else range(nchunk))
    starts_sequence = (g + nsteps - 1) % nsteps == 0
    zs = [jnp.where(starts_sequence, 0.0, z_ref[q]) for q in range(ngroup)]
    pending = list(order)

    def scan_one_chunk():
        if not pending:
            return
        c = pending.pop(0)
        ys = []
        for q in range(ngroup):
            both = jnp.dot(rp_ref[c, q], st(zs[q]), preferred_element_type=F32)
            ys.append(both[0:CHUNK] + yh_ref[c, q])
            zs[q] = both[CHUNK:] + ql_ref[c, q]
        y_ref[c * CHUNK:(c + 1) * CHUNK, :] = jnp.concatenate(ys, axis=1)

    cums = [sum(jnp.dot(tri, p, preferred_element_type=F32) for p in _split(logw[rows[c]], 3))
            for c in range(nchunk)]
    dec = []
    for c in range(nchunk):
        cum, lw = cums[c], logw[rows[c]]
        cmid = cum[CHUNK // 2:CHUNK // 2 + 1]
        ctot = cum[0:1] if reverse else cum[CHUNK - 1:CHUNK]
        e1 = jnp.exp(cum - cmid)
        e1i = jnp.exp(cmid - cum)
        ee = jnp.exp(ctot - cum)
        emid = jnp.exp(cmid)
        r_g = r[rows[c]] * e1
        a_g = -kk[rows[c]] * (e1 * jnp.exp(-lw))
        dec.append(dict(
            r_g=r_g, a_g=a_g, b_g=bb[rows[c]] * e1i, k_g=kd[rows[c]] * e1i, r_s=r_g * emid,
            a_s=a_g * emid, b_e=bb[rows[c]] * ee, k_e=kd[rows[c]] * ee, v=v[rows[c]],
            wtot=jnp.exp(ctot)))

    st_v = [st(dec[c]["v"][:, lanes[q]]) for c, q in chains]
    grams = [_bdot_nt(jnp.concatenate([dec[c]["a_g"][:, lanes[q]], dec[c]["r_g"][:, lanes[q]]], axis=0),
                      jnp.concatenate([st(dec[c]["b_g"][:, lanes[q]]), st(dec[c]["k_g"][:, lanes[q]])],
                                      axis=0)) for c, q in chains]
    scan_one_chunk()
    a_ab = [jnp.where(strict, g_[0:CHUNK, 0:MXU_DIM], 0.0) for g_ in grams]
    a_kr = [jnp.concatenate([jnp.where(strict, g_[0:CHUNK, MXU_DIM:], 0.0),
                             jnp.where(incl, g_[CHUNK:, MXU_DIM:], 0.0)], axis=0) for g_ in grams]
    a_rb = [jnp.where(incl, g_[CHUNK:, 0:MXU_DIM], 0.0) for g_ in grams]
    tinv = [eye_lc + a_ for a_ in a_ab]
    apow = [_bdot(a_, st(a_)) for a_ in a_ab]
    scan_one_chunk()
    npow = CHUNK.bit_length() - 2
    for jj in range(npow):
        if jj < npow - 1:
            both = [_bdot(jnp.concatenate([ap, ti_], axis=0), st(ap)) for ap, ti_ in zip(apow, tinv)]
            apow = [b_[0:CHUNK] for b_ in both]
            tinv = [ti_ + b_[CHUNK:] for ti_, b_ in zip(tinv, both)]
        else:
            tinv = [ti_ + _bdot(ti_, st(ap)) for ap, ti_ in zip(apow, tinv)]
        scan_one_chunk()
    xy = [_bdot(a_kr[i], st_v[i]) for i in range(len(chains))]
    au = [_bdot(tinv[i], jnp.concatenate([st(dec[c]["a_s"][:, lanes[q]]), st(xy[i][0:CHUNK])], axis=1))
          for i, (c, q) in enumerate(chains)]
    while pending:
        scan_one_chunk()
    st_au = [jnp.concatenate([st(x[:, 0:MXU_DIM]), st(x[:, MXU_DIM:])], axis=1) for x in au]
    ry = [_bdot(a_rb[i], st_au[i]) for i in range(len(chains))]
    pq = [_bdot(_head_transpose(dec[c]["b_e"][:, lanes[q]]), st_au[i]) for i, (c, q) in enumerate(chains)]
    q2 = [_bdot(_head_transpose(dec[c]["k_e"][:, lanes[q]]), st_v[i]) for i, (c, q) in enumerate(chains)]
    for q in range(ngroup):
        z_ref[q] = zs[q]
    for i, (c, q) in enumerate(chains):
        r_hat = dec[c]["r_s"][:, lanes[q]] + ry[i][:, 0:MXU_DIM]
        p_lc = eye_lc * dec[c]["wtot"][:, lanes[q]] + pq[i][:, 0:MXU_DIM]
        rp_ref[c, q] = jnp.concatenate([r_hat, p_lc], axis=0).astype(BF16)
        yh_ref[c, q] = ry[i][:, MXU_DIM:] + xy[i][CHUNK:]
        ql_ref[c, q] = pq[i][:, MXU_DIM:] + q2[i]


_UM_WL_BLOCK = 3 * RW_WIDTH // LANES
_UM_AL_BLOCK = _UM_WL_BLOCK + 1
_UM_GL_BLOCK = _UM_WL_BLOCK + 2


def _wkv(um, wp, reverse):
    b, t, _ = um.shape
    ts = min(WKV_STEP, t)
    nsteps = t // ts
    nblocks = b * nsteps
    nchunk = ts // CHUNK
    ngroup = RW_WIDTH // MXU_DIM
    pos = (lambda si: nsteps - 1 - si) if reverse else (lambda si: si)

    def block_of(gi, col):
        return (gi // nsteps, pos(gi % nsteps), col)

    const = lambda g: (0, 0)
    vec = pl.BlockSpec((1, RW_WIDTH), const)
    wide = lambda col: pl.BlockSpec((None, ts, RW_WIDTH),
                                    lambda g: block_of(jnp.minimum(g, nblocks - 1), col))
    narrow = lambda col: pl.BlockSpec((None, ts, LANES),
                                      lambda g: block_of(jnp.minimum(g, nblocks - 1), col))
    return pl.pallas_call(
        functools.partial(_wkv_kernel, reverse=reverse, nsteps=nsteps),
        grid=(nblocks + 1,),
        in_specs=[
            wide(0), wide(1), wide(2), narrow(_UM_WL_BLOCK), narrow(_UM_AL_BLOCK),
            vec,
            pl.BlockSpec((2 * DECAY_LORA, RW_WIDTH), const),
            vec,
            pl.BlockSpec((2 * AAA_LORA, RW_WIDTH), const),
            vec, vec,
            pl.BlockSpec((MXU_DIM, MXU_DIM), const),
        ],
        out_specs=pl.BlockSpec((None, ts, RW_WIDTH), lambda g: block_of(jnp.maximum(g - 1, 0), 0)),
        out_shape=jax.ShapeDtypeStruct((b, t, RW_WIDTH), F32),
        scratch_shapes=[pltpu.VMEM((ngroup, HEAD_DIM, MXU_DIM), F32),
                        pltpu.VMEM((nchunk, ngroup, 2 * CHUNK, MXU_DIM), BF16),
                        pltpu.VMEM((nchunk, ngroup, CHUNK, MXU_DIM), F32),
                        pltpu.VMEM((nchunk, ngroup, HEAD_DIM, MXU_DIM), F32)],
        compiler_params=_params("arbitrary"),
        name="wkv_bwd" if reverse else "wkv_fwd",
    )(um, um, um, um, um, wp["w0"], wp["wup"], wp["a0"], wp["aup"], wp["k_k"], wp["k_a"], wp["hm"])


def _outproj_kernel(h_ref, yna_ref, y0_ref, y1_ref, r_ref, k_ref, v_ref, al_ref, gl_ref,
                    a00_ref, aup0_ref, a01_ref, aup1_ref, ka_ref, rk_ref, gup_ref, lw_ref, lb_ref,
                    hm_ref, w_ref, o_ref):
    hm = hm_ref[...]
    wkv = y0_ref[...] + y1_ref[...]
    mean = _head_sum(wkv, hm) * (1.0 / HEAD_DIM)
    dev = wkv - mean
    var = _head_sum(dev * dev, hm) * (1.0 / HEAD_DIM)
    yn = dev * lax.rsqrt(var + LNX_EPS) * lw_ref[...] + lb_ref[...]
    al = al_ref[...]
    a_sum = (_sigmoid(a00_ref[...] + _bdot(al, aup0_ref[...]))
             + _sigmoid(a01_ref[...] + _bdot(al, aup1_ref[...])))
    kd_sum = k_ref[...] * (2.0 + (a_sum - 2.0) * ka_ref[...])
    bonus = _head_sum(r_ref[...] * kd_sum * rk_ref[...], hm) * v_ref[...]
    gate = _bdot(_sigmoid(gl_ref[...]), gup_ref[...])
    y_rw = (yn + bonus) * gate
    mix = jnp.concatenate([yna_ref[...], y_rw.astype(BF16)], axis=1)
    o_ref[...] = h_ref[...] + jnp.dot(mix, w_ref[...], preferred_element_type=F32)


def _outproj(h, yna, y0, y1, um, lw):
    m, d = h.shape
    tm = min(ROW_TILE, m)
    row = lambda i: (i, 0)
    const = lambda i: (0, 0)
    tok = pl.BlockSpec((tm, RW_WIDTH), row)
    vec = pl.BlockSpec((1, RW_WIDTH), const)
    um_wide = lambda col: pl.BlockSpec((tm, RW_WIDTH), lambda i: (i, col))
    um_narrow = lambda col: pl.BlockSpec((tm, LANES), lambda i: (i, col))
    lora = pl.BlockSpec((2 * AAA_LORA, RW_WIDTH), const)
    d0, d1 = lw["dirs"]
    return pl.pallas_call(
        _outproj_kernel,
        grid=(m // tm,),
        in_specs=[pl.BlockSpec((tm, d), row), pl.BlockSpec((tm, NA_WIDTH), row), tok, tok,
                  um_wide(0), um_wide(1), um_wide(2), um_narrow(_UM_AL_BLOCK), um_narrow(_UM_GL_BLOCK),
                  vec, lora, vec, lora, vec, vec, pl.BlockSpec((GATE_LORA, RW_WIDTH), const), vec, vec,
                  pl.BlockSpec((MXU_DIM, MXU_DIM), const),
                  pl.BlockSpec((NA_WIDTH + RW_WIDTH, d), const)],
        out_specs=pl.BlockSpec((tm, d), row),
        out_shape=jax.ShapeDtypeStruct((m, d), F32),
        compiler_params=_params("parallel"),
        name="outproj",
    )(h, yna, y0, y1, um, um, um, um, um, d0["a0"], d0["aup"], d1["a0"], d1["aup"], d0["k_a"],
      lw["r_k"], lw["gup"], lw["lnx_w"], lw["lnx_b"], lw["hm"], lw["w_out"])


def _pad_lora(w_up):
    zero = jnp.zeros_like(w_up[0])
    return (jnp.concatenate([w_up[0], zero], axis=0).astype(BF16),
            jnp.concatenate([zero, w_up[1]], axis=0).astype(BF16))


def _layer_weights(w, i):
    row = lambda x: x.reshape(1, -1)
    wup = _pad_lora(w["rw_w_up"][i])
    aup = _pad_lora(w["rw_a_up"][i])
    lane = jnp.arange(MXU_DIM)
    hm = (lane[:, None] // HEAD_DIM == lane[None, :] // HEAD_DIM).astype(BF16)
    common = dict(k_k=row(w["rw_k_k"][i]), k_a=row(w["rw_k_a"][i]), hm=hm)
    dirs = [dict(common, w0=row(w["rw_w0"][i, d]), wup=wup[d], a0=row(w["rw_a0"][i, d]), aup=aup[d])
            for d in range(2)]
    w_in = w["w_in"][i].astype(BF16)
    return dict(
        ffn1=(row(w["ffn1_norm"][i]), w["ffn1_wg"][i].astype(BF16), w["ffn1_wu"][i].astype(BF16),
              w["ffn1_wd"][i].astype(BF16)),
        ffn2=(row(w["ffn2_norm"][i]), w["ffn2_wg"][i].astype(BF16), w["ffn2_wu"][i].astype(BF16),
              w["ffn2_wd"][i].astype(BF16)),
        mix_norm=row(w["mix_norm"][i]), w_qkv=w_in[:, :3 * NA_WIDTH], w_rw=w_in[:, 3 * NA_WIDTH:],
        mu=row(w["rw_mu"][i]), tz=_na_bias_table(w["na_rpb"][i]), dirs=dirs, hm=hm,
        r_k=row(w["rw_r_k"][i]), gup=w["rw_g_up"][i].astype(BF16),
        lnx_w=row(w["rw_lnx_w"][i]), lnx_b=row(w["rw_lnx_b"][i]), w_out=w["w_out"][i].astype(BF16),
        ple=(row(w["ple_norm"][i]), w["ple_gate"][i].astype(BF16), w["ple_up"][i].astype(BF16)),
    )


def _trunk(x, p, layers, final_norm):
    b, t, d = x.shape
    m = b * t
    h = x.reshape(m, d)
    depth = len(layers)
    for i, lw in enumerate(layers):
        h = _ffn(h, *lw["ffn1"])
        qkv, um = _inproj(h, lw["mix_norm"], lw["w_qkv"], lw["w_rw"], lw["mu"], seq_len=t)
        yna = _na(qkv.reshape(b, t, -1), lw["tz"])
        um3 = um.reshape(b, t, -1)
        y0 = _wkv(um3, lw["dirs"][0], reverse=False)
        y1 = _wkv(um3, lw["dirs"][1], reverse=True)
        h = _outproj(h, yna.reshape(m, -1), y0.reshape(m, -1), y1.reshape(m, -1), um, lw)
        pn, pg, pu = lw["ple"]
        h = _ffn(h, *lw["ffn2"], ple=(p[i].reshape(m, -1), pn, pg, pu),
                 final_norm=final_norm if i == depth - 1 else None)
    return h.reshape(b, t, d)


def kernel(x_prompt, x_sample, p_prompt, p_sample, ffn1_norm, ffn1_wg, ffn1_wu, ffn1_wd, mix_norm, w_in, na_rpb, rw_mu, rw_w0, rw_w_up, rw_a0, rw_a_up, rw_g_up, rw_k_k, rw_k_a, rw_r_k, rw_lnx_w, rw_lnx_b, w_out, ffn2_norm, ffn2_wg, ffn2_wu, ffn2_wd, ple_norm, ple_gate, ple_up, final_norm):
    w = dict(ffn1_norm=ffn1_norm, ffn1_wg=ffn1_wg, ffn1_wu=ffn1_wu, ffn1_wd=ffn1_wd, mix_norm=mix_norm,
             w_in=w_in, na_rpb=na_rpb, rw_mu=rw_mu, rw_w0=rw_w0, rw_w_up=rw_w_up, rw_a0=rw_a0,
             rw_a_up=rw_a_up, rw_g_up=rw_g_up, rw_k_k=rw_k_k, rw_k_a=rw_k_a, rw_r_k=rw_r_k,
             rw_lnx_w=rw_lnx_w, rw_lnx_b=rw_lnx_b, w_out=w_out, ffn2_norm=ffn2_norm, ffn2_wg=ffn2_wg,
             ffn2_wu=ffn2_wu, ffn2_wd=ffn2_wd, ple_norm=ple_norm, ple_gate=ple_gate, ple_up=ple_up)
    layers = [_layer_weights(w, i) for i in range(ffn1_wg.shape[0])]
    fn = final_norm.reshape(1, -1)
    return (_trunk(x_prompt, p_prompt, layers, fn), _trunk(x_sample, p_sample, layers, fn))
```

```python
import functools

import jax
import jax.numpy as jnp
from jax import lax
from jax.experimental import pallas as pl
from jax.experimental.pallas import tpu as pltpu

F32 = jnp.float32
BF16 = jnp.bfloat16

GRID_W = 64
HEAD_DIM = 64
NA_HEADS = 8
RW_HEADS = 8
NA_WIDTH = NA_HEADS * HEAD_DIM
RW_WIDTH = RW_HEADS * HEAD_DIM
WIN_ROWS = 8
WIN_COLS = 16
DECAY_LORA = 64
AAA_LORA = 64
GATE_LORA = 128
RW_COLS = 3 * RW_WIDTH + 2 * DECAY_LORA + 2 * AAA_LORA + GATE_LORA
NORM_EPS = 1e-6
LNX_EPS = 64e-5
DECAY_SCALE = 0.606531
MASK_VALUE = -1e30

LANES = 128
MXU_DIM = 256
VMEM_LIMIT_BYTES = 52 * 1024 * 1024

ROW_TILE = 512
FFN_SUBTILES = 2
NA_ROWS = 8
NA_UNROLL = 4
CHUNK = 64
WKV_STEP = 256
GROUP_HEADS = MXU_DIM // HEAD_DIM


def _params(*sem):
    return pltpu.CompilerParams(dimension_semantics=sem, vmem_limit_bytes=VMEM_LIMIT_BYTES)


def _bdot(a, b):
    return jnp.dot(a.astype(BF16), b.astype(BF16), preferred_element_type=F32)


def _split(x, n):
    parts = []
    rem = x
    for _ in range(n):
        p = rem.astype(BF16)
        parts.append(p)
        rem = rem - p.astype(F32)
    return parts


def _rms(x, g):
    ms = jnp.mean(x * x, axis=-1, keepdims=True)
    return x * lax.rsqrt(ms + NORM_EPS) * g


def _sigmoid(x):
    return 1.0 / (1.0 + jnp.exp(-x))


def _ffn_kernel(*refs, with_ple, with_final):
    if with_ple:
        x_ref, g_ref, wg_ref, wu_ref, wd_ref, p_ref, pn_ref, pg_ref, pu_ref, fn_ref, o_ref = refs
    else:
        x_ref, g_ref, wg_ref, wu_ref, wd_ref, o_ref = refs
    sub = x_ref.shape[0] // FFN_SUBTILES
    rows = [slice(i * sub, (i + 1) * sub) for i in range(FFN_SUBTILES)]
    xns = [_rms(x_ref[r], g_ref[...]).astype(BF16) for r in rows]
    gates = [jnp.dot(xn, wg_ref[...], preferred_element_type=F32) for xn in xns]
    ups = [jnp.dot(xn, wu_ref[...], preferred_element_type=F32) for xn in xns]
    mids = [((a * _sigmoid(a)) * b).astype(BF16) for a, b in zip(gates, ups)]
    downs = [jnp.dot(m, wd_ref[...], preferred_element_type=F32) for m in mids]
    hs = [x_ref[r] + 0.5 * d for r, d in zip(rows, downs)]
    if with_ple:
        pgate = [_sigmoid(_bdot(_rms(h, pn_ref[...]), pg_ref[...])) for h in hs]
        pup = [_bdot(p_ref[r], pu_ref[...]) for r in rows]
        hs = [h + gt * up for h, gt, up in zip(hs, pgate, pup)]
        if with_final:
            hs = [_rms(h, fn_ref[...]) for h in hs]
    for r, h in zip(rows, hs):
        o_ref[r] = h


def _ffn(x, g, wg, wu, wd, ple=None, final_norm=None):
    m, d = x.shape
    dff = wg.shape[1]
    tm = min(ROW_TILE, m)
    row = lambda i: (i, 0)
    const = lambda i: (0, 0)
    resident = lambda shape: pl.BlockSpec(shape, const, pipeline_mode=pl.Buffered(1))
    in_specs = [pl.BlockSpec((tm, d), row), pl.BlockSpec((1, d), const),
                resident((d, dff)), resident((d, dff)), resident((dff, d))]
    args = [x, g, wg, wu, wd]
    if ple is not None:
        p, pn, pg, pu = ple
        fn = final_norm if final_norm is not None else pn
        in_specs += [pl.BlockSpec((tm, p.shape[1]), row), pl.BlockSpec((1, d), const),
                     resident((d, d)), resident((p.shape[1], d)), pl.BlockSpec((1, d), const)]
        args += [p, pn, pg, pu, fn]
    kern = functools.partial(_ffn_kernel, with_ple=ple is not None, with_final=final_norm is not None)
    return pl.pallas_call(
        kern,
        grid=(m // tm,),
        in_specs=in_specs,
        out_specs=pl.BlockSpec((tm, d), row),
        out_shape=jax.ShapeDtypeStruct((m, d), F32),
        compiler_params=_params("parallel"),
        name="ffn",
    )(*args)


def _inproj_kernel(x_ref, xp_ref, xn_ref, g_ref, wq_ref, wr_ref, mu_ref, qkv_ref, um_ref, *, seq_len):
    i = pl.program_id(0)
    tm = x_ref.shape[0]
    half = tm // 2
    g = g_ref[...]
    lane = lax.broadcasted_iota(jnp.int32, (1, 3 * NA_WIDTH), 1)
    scale = jnp.where(lane < NA_WIDTH, HEAD_DIM ** -0.5, 1.0)
    has_prev = (i * tm) % seq_len != 0
    has_next = ((i + 1) * tm) % seq_len != 0
    xn = [_rms(x_ref[0:half], g).astype(BF16), _rms(x_ref[half:tm], g).astype(BF16)]
    before = [_rms(xp_ref[...], g).astype(BF16), _rms(x_ref[half - 8:half], g).astype(BF16)]
    after = [_rms(x_ref[half:half + 8], g).astype(BF16), _rms(xn_ref[...], g).astype(BF16)]
    keep_before = [has_prev, True]
    keep_after = [True, has_next]
    for s in range(2):
        rows = slice(s * half, (s + 1) * half)
        proj = jnp.dot(xn[s], wq_ref[...], preferred_element_type=F32)
        qkv_ref[rows] = (proj * scale).astype(BF16)
        ext = jnp.concatenate([before[s], xn[s], after[s]], axis=0)
        u = jnp.dot(ext, wr_ref[...], preferred_element_type=F32)
        u = jnp.concatenate([jnp.where(keep_before[s], u[0:8], 0.0), u[8:half + 8],
                             jnp.where(keep_after[s], u[half + 8:], 0.0)], axis=0)
        nbr = pltpu.roll(u, 1, axis=0) + pltpu.roll(u, half + 15, axis=0)
        uc = u[8:half + 8]
        um_ref[rows] = uc + mu_ref[...] * (0.5 * nbr[8:half + 8] - uc)


def _inproj(x, g, w_qkv, w_rw, mu, seq_len):
    m, d = x.shape
    tm = min(ROW_TILE, m, seq_len)
    na3 = 3 * NA_WIDTH
    hb = tm // 8
    nhb = m // 8
    const = lambda i: (0, 0)
    return pl.pallas_call(
        functools.partial(_inproj_kernel, seq_len=seq_len),
        grid=(m // tm,),
        in_specs=[
            pl.BlockSpec((tm, d), lambda i: (i, 0)),
            pl.BlockSpec((8, d), lambda i: (jnp.maximum(i * hb - 1, 0), 0)),
            pl.BlockSpec((8, d), lambda i: (jnp.minimum((i + 1) * hb, nhb - 1), 0)),
            pl.BlockSpec((1, d), const),
            pl.BlockSpec((d, na3), const),
            pl.BlockSpec((d, RW_COLS), const),
            pl.BlockSpec((1, RW_COLS), const),
        ],
        out_specs=[
            pl.BlockSpec((tm, na3), lambda i: (i, 0)),
            pl.BlockSpec((tm, RW_COLS), lambda i: (i, 0)),
        ],
        out_shape=[jax.ShapeDtypeStruct((m, na3), BF16), jax.ShapeDtypeStruct((m, RW_COLS), F32)],
        compiler_params=_params("parallel"),
        name="inproj",
    )(x, x, x, g, w_qkv, w_rw, mu)


def _na_bias_table(rpb):
    j = jnp.arange(GRID_W)
    c = jnp.arange(GRID_W)
    cs = jnp.clip(j - WIN_COLS // 2, 0, GRID_W - WIN_COLS)
    valid = (c[None, :] >= cs[:, None]) & (c[None, :] < cs[:, None] + WIN_COLS)
    cidx = jnp.clip(c[None, :] - j[:, None] + (WIN_COLS - 1), 0, 2 * WIN_COLS - 2)
    full = jnp.where(valid, rpb[:, :, cidx], MASK_VALUE)
    nri = 2 * WIN_ROWS - 2
    two = jnp.concatenate([full[:, 0:nri], full[:, 1:nri + 1]], axis=-1)
    two = two.reshape(NA_HEADS // 2, 2, nri, GRID_W, 2 * GRID_W)
    return jnp.transpose(two, (0, 2, 1, 3, 4)).reshape(NA_HEADS // 2, nri, 2 * GRID_W, 2 * GRID_W)


def _na_kernel(q_ref, kp_ref, kc_ref, kn_ref, vp_ref, vc_ref, vn_ref, tz_ref, o_ref,
               kbuf, vbuf, *, rows):
    qb = pl.program_id(1)
    blk = NA_ROWS * GRID_W
    kbuf[0:blk] = kp_ref[...]
    kbuf[blk:2 * blk] = kc_ref[...]
    kbuf[2 * blk:3 * blk] = kn_ref[...]
    vbuf[0:blk] = vp_ref[...]
    vbuf[blk:2 * blk] = vc_ref[...]
    vbuf[2 * blk:3 * blk] = vn_ref[...]
    lane = lax.broadcasted_iota(jnp.int32, (GRID_W, 2 * HEAD_DIM), 1)
    low = lane < HEAD_DIM
    win = WIN_ROWS * GRID_W

    npair = NA_HEADS // 2
    sls = [slice(hp * 2 * HEAD_DIM, (hp + 1) * 2 * HEAD_DIM) for hp in range(npair)]
    zero = jnp.zeros((GRID_W, 2 * HEAD_DIM), BF16)

    def body(it, carry):
        qoff, koff, delta = [], [], []
        for n in range(NA_UNROLL):
            il = it * NA_UNROLL + n
            i = qb * NA_ROWS + il
            rs = jnp.clip(i - WIN_ROWS // 2, 0, rows - WIN_ROWS)
            delta.append(i - rs)
            koff.append(pl.multiple_of((rs - qb * NA_ROWS + NA_ROWS) * GRID_W, GRID_W))
            qoff.append(pl.multiple_of(il * GRID_W, GRID_W))
        chains = [(n, hp) for n in range(NA_UNROLL) for hp in range(npair)]
        qps = [q_ref[pl.ds(qoff[n], GRID_W), sls[hp]] for n, hp in chains]
        qsts = [jnp.concatenate([jnp.where(low, qp, zero), jnp.where(low, zero, qp)], axis=0)
                for qp in qps]
        ss = [lax.dot_general(qsts[ci], kbuf[pl.ds(koff[n], win), sls[hp]], (((1,), (1,)), ((), ())),
                              preferred_element_type=F32)
              for ci, (n, hp) in enumerate(chains)]
        ss = [ss[ci] + jnp.concatenate(
            [tz_ref[hp, 2 * p - delta[n] + (WIN_ROWS - 1)] for p in range(WIN_ROWS // 2)], axis=1)
            for ci, (n, hp) in enumerate(chains)]
        ps = [jnp.exp(s - jnp.max(s, axis=1, keepdims=True)) for s in ss]
        ps = [p * (1.0 / jnp.sum(p, axis=1, keepdims=True)) for p in ps]
        os_ = [jnp.dot(ps[ci].astype(BF16), vbuf[pl.ds(koff[n], win), sls[hp]],
                       preferred_element_type=F32) for ci, (n, hp) in enumerate(chains)]
        outs = [jnp.where(low, o[0:GRID_W], o[GRID_W:2 * GRID_W]) for o in os_]
        for n in range(NA_UNROLL):
            o_ref[pl.ds(qoff[n], GRID_W), :] = jnp.concatenate(
                outs[n * npair:(n + 1) * npair], axis=1).astype(BF16)
        return carry

    lax.fori_loop(0, NA_ROWS // NA_UNROLL, body, 0)


def _na(qkv, tz):
    b, t, _ = qkv.shape
    rows = t // GRID_W
    nqb = rows // NA_ROWS
    blk = NA_ROWS * GRID_W
    spec = lambda col, shift: pl.BlockSpec(
        (None, blk, NA_WIDTH),
        lambda bi, qi: (bi, jnp.clip(qi + shift, 0, nqb - 1), col))
    return pl.pallas_call(
        functools.partial(_na_kernel, rows=rows),
        grid=(b, nqb),
        in_specs=[spec(0, 0), spec(1, -1), spec(1, 0), spec(1, 1), spec(2, -1), spec(2, 0), spec(2, 1),
                  pl.BlockSpec(tz.shape, lambda bi, qi: (0, 0, 0, 0))],
        out_specs=pl.BlockSpec((None, blk, NA_WIDTH), lambda bi, qi: (bi, qi, 0)),
        out_shape=jax.ShapeDtypeStruct((b, t, NA_WIDTH), BF16),
        scratch_shapes=[pltpu.VMEM((3 * blk, NA_WIDTH), BF16), pltpu.VMEM((3 * blk, NA_WIDTH), BF16)],
        compiler_params=_params("parallel", "arbitrary"),
        name="natten",
    )(qkv, qkv, qkv, qkv, qkv, qkv, qkv, tz)


def _head_sum(x, hm):
    outs = []
    for q in range(RW_WIDTH // MXU_DIM):
        xs = x[:, q * MXU_DIM:(q + 1) * MXU_DIM]
        outs.append(sum(jnp.dot(p, hm, preferred_element_type=F32) for p in _split(xs, 2)))
    return jnp.concatenate(outs, axis=1)


def _stack(x, low):
    xb = x.astype(BF16)
    zero = jnp.zeros((x.shape[0], LANES), BF16)
    blocks = []
    for h in range(GROUP_HEADS):
        tile = xb[:, (h // 2) * LANES:(h // 2 + 1) * LANES]
        kept = jnp.where(low, tile, zero) if h % 2 == 0 else jnp.where(low, zero, tile)
        blocks.append(jnp.concatenate([kept, zero] if h < 2 else [zero, kept], axis=1))
    return jnp.concatenate(blocks, axis=0)


def _head_transpose_pair(x0, x1, low):
    xt = jnp.concatenate([x0, x1], axis=0).T
    b = [xt[h * HEAD_DIM:(h + 1) * HEAD_DIM] for h in range(GROUP_HEADS)]
    rolled = [pltpu.roll(bh, HEAD_DIM, axis=1) for bh in b]
    y0 = jnp.concatenate([jnp.where(low, b[0], rolled[1]), jnp.where(low, b[2], rolled[3])], axis=1)
    y1 = jnp.concatenate([jnp.where(low, rolled[0], b[1]), jnp.where(low, rolled[2], b[3])], axis=1)
    return y0, y1


def _wkv_kernel(r_ref, k_ref, v_ref, wl_ref, al_ref, w0_ref, wup_ref, a0_ref, aup_ref,
                kk_ref, ka_ref, hm_ref, y_ref, z_ref, rp_ref, yh_ref, ql_ref, *, reverse, nsteps):
    g = pl.program_id(0)
    ts = r_ref.shape[0]
    nchunk = ts // CHUNK
    ngroup = RW_WIDTH // MXU_DIM

    @pl.when(g == 0)
    def _():
        z_ref[...] = jnp.zeros_like(z_ref)
        rp_ref[...] = jnp.zeros_like(rp_ref)
        yh_ref[...] = jnp.zeros_like(yh_ref)
        ql_ref[...] = jnp.zeros_like(ql_ref)

    r = r_ref[...]
    k = k_ref[...]
    v = v_ref[...]
    logw = -DECAY_SCALE * _sigmoid(w0_ref[...] + _bdot(jnp.tanh(wl_ref[...]), wup_ref[...]))
    a = _sigmoid(a0_ref[...] + _bdot(al_ref[...], aup_ref[...]))
    kk = k * kk_ref[...]
    kk = kk / jnp.maximum(jnp.sqrt(_head_sum(kk * kk, hm_ref[...])), 1e-12)
    kd = k * (1.0 + (a - 1.0) * ka_ref[...])
    bb = kk * a

    ti = lax.broadcasted_iota(jnp.int32, (CHUNK, CHUNK), 0)
    si = lax.broadcasted_iota(jnp.int32, (CHUNK, CHUNK), 1)
    tri = jnp.where((si >= ti) if reverse else (si <= ti), 1.0, 0.0).astype(BF16)
    lane = lax.broadcasted_iota(jnp.int32, (CHUNK, MXU_DIM), 1)
    trow = lax.broadcasted_iota(jnp.int32, (CHUNK, MXU_DIM), 0)
    s_of_lane = lane % HEAD_DIM
    if reverse:
        strict, incl = s_of_lane > trow, s_of_lane >= trow
    else:
        strict, incl = s_of_lane < trow, s_of_lane <= trow
    eye_lc = jnp.where(s_of_lane == trow, 1.0, 0.0)
    low = lax.broadcasted_iota(jnp.int32, (CHUNK, LANES), 1) < HEAD_DIM

    chains = [(c, q) for c in range(nchunk) for q in range(ngroup)]
    lanes = [slice(q * MXU_DIM, (q + 1) * MXU_DIM) for q in range(ngroup)]
    rows = [slice(c * CHUNK, (c + 1) * CHUNK) for c in range(nchunk)]
    st = lambda x: _stack(x, low)

    order = list(range(nchunk - 1, -1, -1) if reverse else range(nchunk))
    starts_sequence = (g + nsteps - 1) % nsteps == 0
    zs = [jnp.where(starts_sequence, 0.0, z_ref[q]) for q in range(ngroup)]
    pending = list(order)

    def scan_one_chunk():
        if not pending:
            return
        c = pending.pop(0)
        ys = []
        for q in range(ngroup):
            both = jnp.dot(rp_ref[c, q], st(zs[q]), preferred_element_type=F32)
            ys.append(both[0:CHUNK] + yh_ref[c, q])
            zs[q] = both[CHUNK:] + ql_ref[c, q]
        y_ref[c * CHUNK:(c + 1) * CHUNK, :] = jnp.concatenate(ys, axis=1)

    cums = [sum(jnp.dot(tri, p, preferred_element_type=F32) for p in _split(logw[rows[c]], 3))
            for c in range(nchunk)]
    dec = []
    for c in range(nchunk):
        cum, lw = cums[c], logw[rows[c]]
        cmid = cum[CHUNK // 2:CHUNK // 2 + 1]
        ctot = cum[0:1] if reverse else cum[CHUNK - 1:CHUNK]
        e1 = jnp.exp(cum - cmid)
        e1i = jnp.exp(cmid - cum)
        ee = jnp.exp(ctot - cum)
        emid = jnp.exp(cmid)
        r_g = r[rows[c]] * e1
        a_g = -kk[rows[c]] * (e1 * jnp.exp(-lw))
        dec.append(dict(
            r_g=r_g, a_g=a_g, b_g=bb[rows[c]] * e1i, k_g=kd[rows[c]] * e1i, r_s=r_g * emid,
            a_s=a_g * emid, b_e=bb[rows[c]] * ee, k_e=kd[rows[c]] * ee, v=v[rows[c]],
            wtot=jnp.exp(ctot)))

    tr = {}
    for name in ("b_g", "k_g", "b_e", "k_e"):
        for c in range(0, nchunk, 2):
            for q in range(ngroup):
                tr[name, c, q], tr[name, c + 1, q] = _head_transpose_pair(
                    dec[c][name][:, lanes[q]], dec[c + 1][name][:, lanes[q]], low)

    st_v = [st(dec[c]["v"][:, lanes[q]]) for c, q in chains]
    grams = [_bdot(jnp.concatenate([dec[c]["a_g"][:, lanes[q]], dec[c]["r_g"][:, lanes[q]]], axis=0),
                   jnp.concatenate([st(tr["b_g", c, q]), st(tr["k_g", c, q])], axis=1))
             for c, q in chains]
    scan_one_chunk()
    a_ab = [jnp.where(strict, g_[0:CHUNK, 0:MXU_DIM], 0.0) for g_ in grams]
    a_kr = [jnp.concatenate([jnp.where(strict, g_[0:CHUNK, MXU_DIM:], 0.0),
                             jnp.where(incl, g_[CHUNK:, MXU_DIM:], 0.0)], axis=0) for g_ in grams]
    a_rb = [jnp.where(incl, g_[CHUNK:, 0:MXU_DIM], 0.0) for g_ in grams]
    tinv = [eye_lc + a_ for a_ in a_ab]
    apow = [_bdot(a_, st(a_)) for a_ in a_ab]
    scan_one_chunk()
    npow = CHUNK.bit_length() - 2
    for jj in range(npow):
        if jj < npow - 1:
            both = [_bdot(jnp.concatenate([ap, ti_], axis=0), st(ap)) for ap, ti_ in zip(apow, tinv)]
            apow = [b_[0:CHUNK] for b_ in both]
            tinv = [ti_ + b_[CHUNK:] for ti_, b_ in zip(tinv, both)]
        else:
            tinv = [ti_ + _bdot(ti_, st(ap)) for ap, ti_ in zip(apow, tinv)]
        scan_one_chunk()
    xyq = [_bdot(jnp.concatenate([a_kr[i], tr["k_e", c, q]], axis=0), st_v[i])
           for i, (c, q) in enumerate(chains)]
    xy = [x[0:2 * CHUNK] for x in xyq]
    q2 = [x[2 * CHUNK:] for x in xyq]
    au = [_bdot(tinv[i], jnp.concatenate([st(dec[c]["a_s"][:, lanes[q]]), st(xy[i][0:CHUNK])], axis=1))
          for i, (c, q) in enumerate(chains)]
    while pending:
        scan_one_chunk()
    st_au = [jnp.concatenate([st(x[:, 0:MXU_DIM]), st(x[:, MXU_DIM:])], axis=1) for x in au]
    rypq = [_bdot(jnp.concatenate([a_rb[i], tr["b_e", c, q]], axis=0), st_au[i])
            for i, (c, q) in enumerate(chains)]
    ry = [x[0:CHUNK] for x in rypq]
    pq = [x[CHUNK:] for x in rypq]
    for q in range(ngroup):
        z_ref[q] = zs[q]
    for i, (c, q) in enumerate(chains):
        r_hat = dec[c]["r_s"][:, lanes[q]] + ry[i][:, 0:MXU_DIM]
        p_lc = eye_lc * dec[c]["wtot"][:, lanes[q]] + pq[i][:, 0:MXU_DIM]
        rp_ref[c, q] = jnp.concatenate([r_hat, p_lc], axis=0).astype(BF16)
        yh_ref[c, q] = ry[i][:, MXU_DIM:] + xy[i][CHUNK:]
        ql_ref[c, q] = pq[i][:, MXU_DIM:] + q2[i]


_UM_WL_BLOCK = 3 * RW_WIDTH // LANES
_UM_AL_BLOCK = _UM_WL_BLOCK + 1
_UM_GL_BLOCK = _UM_WL_BLOCK + 2


def _wkv(um, wp, reverse):
    b, t, _ = um.shape
    ts = min(WKV_STEP, t)
    nsteps = t // ts
    nblocks = b * nsteps
    nchunk = ts // CHUNK
    ngroup = RW_WIDTH // MXU_DIM
    pos = (lambda si: nsteps - 1 - si) if reverse else (lambda si: si)

    def block_of(gi, col):
        return (gi // nsteps, pos(gi % nsteps), col)

    const = lambda g: (0, 0)
    vec = pl.BlockSpec((1, RW_WIDTH), const)
    wide = lambda col: pl.BlockSpec((None, ts, RW_WIDTH),
                                    lambda g: block_of(jnp.minimum(g, nblocks - 1), col))
    narrow = lambda col: pl.BlockSpec((None, ts, LANES),
                                      lambda g: block_of(jnp.minimum(g, nblocks - 1), col))
    return pl.pallas_call(
        functools.partial(_wkv_kernel, reverse=reverse, nsteps=nsteps),
        grid=(nblocks + 1,),
        in_specs=[
            wide(0), wide(1), wide(2), narrow(_UM_WL_BLOCK), narrow(_UM_AL_BLOCK),
            vec,
            pl.BlockSpec((2 * DECAY_LORA, RW_WIDTH), const),
            vec,
            pl.BlockSpec((2 * AAA_LORA, RW_WIDTH), const),
            vec, vec,
            pl.BlockSpec((MXU_DIM, MXU_DIM), const),
        ],
        out_specs=pl.BlockSpec((None, ts, RW_WIDTH), lambda g: block_of(jnp.maximum(g - 1, 0), 0)),
        out_shape=jax.ShapeDtypeStruct((b, t, RW_WIDTH), F32),
        scratch_shapes=[pltpu.VMEM((ngroup, HEAD_DIM, MXU_DIM), F32),
                        pltpu.VMEM((nchunk, ngroup, 2 * CHUNK, MXU_DIM), BF16),
                        pltpu.VMEM((nchunk, ngroup, CHUNK, MXU_DIM), F32),
                        pltpu.VMEM((nchunk, ngroup, HEAD_DIM, MXU_DIM), F32)],
        compiler_params=_params("arbitrary"),
        name="wkv_bwd" if reverse else "wkv_fwd",
    )(um, um, um, um, um, wp["w0"], wp["wup"], wp["a0"], wp["aup"], wp["k_k"], wp["k_a"], wp["hm"])


def _outproj_kernel(h_ref, yna_ref, y0_ref, y1_ref, r_ref, k_ref, v_ref, al_ref, gl_ref,
                    a00_ref, aup0_ref, a01_ref, aup1_ref, ka_ref, rk_ref, gup_ref, lw_ref, lb_ref,
                    hm_ref, w_ref, o_ref):
    hm = hm_ref[...]
    half = h_ref.shape[0] // 2
    groups = [slice(0, half), slice(half, 2 * half)]
    wkv = [y0_ref[rows] + y1_ref[rows] for rows in groups]
    mean = [_head_sum(x, hm) * (1.0 / HEAD_DIM) for x in wkv]
    dev = [x - m_ for x, m_ in zip(wkv, mean)]
    var = [_head_sum(d_ * d_, hm) * (1.0 / HEAD_DIM) for d_ in dev]
    yn = [d_ * lax.rsqrt(v_ + LNX_EPS) * lw_ref[...] + lb_ref[...] for d_, v_ in zip(dev, var)]
    a_sum = [_sigmoid(a00_ref[...] + _bdot(al_ref[rows], aup0_ref[...]))
             + _sigmoid(a01_ref[...] + _bdot(al_ref[rows], aup1_ref[...])) for rows in groups]
    kd_sum = [k_ref[rows] * (2.0 + (a_ - 2.0) * ka_ref[...]) for rows, a_ in zip(groups, a_sum)]
    bonus = [_head_sum(r_ref[rows] * kd * rk_ref[...], hm) * v_ref[rows] for rows, kd in zip(groups, kd_sum)]
    gate = [_bdot(_sigmoid(gl_ref[rows]), gup_ref[...]) for rows in groups]
    for rows, yn_, bonus_, gate_ in zip(groups, yn, bonus, gate):
        y_rw = (yn_ + bonus_) * gate_
        mix = jnp.concatenate([yna_ref[rows], y_rw.astype(BF16)], axis=1)
        o_ref[rows] = h_ref[rows] + jnp.dot(mix, w_ref[...], preferred_element_type=F32)


def _outproj(h, yna, y0, y1, um, lw):
    m, d = h.shape
    tm = min(ROW_TILE, m)
    row = lambda i: (i, 0)
    const = lambda i: (0, 0)
    tok = pl.BlockSpec((tm, RW_WIDTH), row)
    vec = pl.BlockSpec((1, RW_WIDTH), const)
    um_wide = lambda col: pl.BlockSpec((tm, RW_WIDTH), lambda i: (i, col))
    um_narrow = lambda col: pl.BlockSpec((tm, LANES), lambda i: (i, col))
    lora = pl.BlockSpec((2 * AAA_LORA, RW_WIDTH), const)
    d0, d1 = lw["dirs"]
    return pl.pallas_call(
        _outproj_kernel,
        grid=(m // tm,),
        in_specs=[pl.BlockSpec((tm, d), row), pl.BlockSpec((tm, NA_WIDTH), row), tok, tok,
                  um_wide(0), um_wide(1), um_wide(2), um_narrow(_UM_AL_BLOCK), um_narrow(_UM_GL_BLOCK),
                  vec, lora, vec, lora, vec, vec, pl.BlockSpec((GATE_LORA, RW_WIDTH), const), vec, vec,
                  pl.BlockSpec((MXU_DIM, MXU_DIM), const),
                  pl.BlockSpec((NA_WIDTH + RW_WIDTH, d), const)],
        out_specs=pl.BlockSpec((tm, d), row),
        out_shape=jax.ShapeDtypeStruct((m, d), F32),
        compiler_params=_params("parallel"),
        name="outproj",
    )(h, yna, y0, y1, um, um, um, um, um, d0["a0"], d0["aup"], d1["a0"], d1["aup"], d0["k_a"],
      lw["r_k"], lw["gup"], lw["lnx_w"], lw["lnx_b"], lw["hm"], lw["w_out"])


def _pad_lora(w_up):
    zero = jnp.zeros_like(w_up[0])
    return (jnp.concatenate([w_up[0], zero], axis=0).astype(BF16),
            jnp.concatenate([zero, w_up[1]], axis=0).astype(BF16))


def _layer_weights(w, i):
    row = lambda x: x.reshape(1, -1)
    wup = _pad_lora(w["rw_w_up"][i])
    aup = _pad_lora(w["rw_a_up"][i])
    lane = jnp.arange(MXU_DIM)
    hm = (lane[:, None] // HEAD_DIM == lane[None, :] // HEAD_DIM).astype(BF16)
    common = dict(k_k=row(w["rw_k_k"][i]), k_a=row(w["rw_k_a"][i]), hm=hm)
    dirs = [dict(common, w0=row(w["rw_w0"][i, d]), wup=wup[d], a0=row(w["rw_a0"][i, d]), aup=aup[d])
            for d in range(2)]
    w_in = w["w_in"][i].astype(BF16)
    return dict(
        ffn1=(row(w["ffn1_norm"][i]), w["ffn1_wg"][i].astype(BF16), w["ffn1_wu"][i].astype(BF16),
              w["ffn1_wd"][i].astype(BF16)),
        ffn2=(row(w["ffn2_norm"][i]), w["ffn2_wg"][i].astype(BF16), w["ffn2_wu"][i].astype(BF16),
              w["ffn2_wd"][i].astype(BF16)),
        mix_norm=row(w["mix_norm"][i]), w_qkv=w_in[:, :3 * NA_WIDTH], w_rw=w_in[:, 3 * NA_WIDTH:],
        mu=row(w["rw_mu"][i]), tz=_na_bias_table(w["na_rpb"][i]), dirs=dirs, hm=hm,
        r_k=row(w["rw_r_k"][i]), gup=w["rw_g_up"][i].astype(BF16),
        lnx_w=row(w["rw_lnx_w"][i]), lnx_b=row(w["rw_lnx_b"][i]), w_out=w["w_out"][i].astype(BF16),
        ple=(row(w["ple_norm"][i]), w["ple_gate"][i].astype(BF16), w["ple_up"][i].astype(BF16)),
    )


def _trunk(x, p, layers, final_norm):
    b, t, d = x.shape
    m = b * t
    h = x.reshape(m, d)
    depth = len(layers)
    for i, lw in enumerate(layers):
        h = _ffn(h, *lw["ffn1"])
        qkv, um = _inproj(h, lw["mix_norm"], lw["w_qkv"], lw["w_rw"], lw["mu"], seq_len=t)
        yna = _na(qkv.reshape(b, t, -1), lw["tz"])
        um3 = um.reshape(b, t, -1)
        y0 = _wkv(um3, lw["dirs"][0], reverse=False)
        y1 = _wkv(um3, lw["dirs"][1], reverse=True)
        h = _outproj(h, yna.reshape(m, -1), y0.reshape(m, -1), y1.reshape(m, -1), um, lw)
        pn, pg, pu = lw["ple"]
        h = _ffn(h, *lw["ffn2"], ple=(p[i].reshape(m, -1), pn, pg, pu),
                 final_norm=final_norm if i == depth - 1 else None)
    return h.reshape(b, t, d)


def kernel(x_prompt, x_sample, p_prompt, p_sample, ffn1_norm, ffn1_wg, ffn1_wu, ffn1_wd, mix_norm, w_in, na_rpb, rw_mu, rw_w0, rw_w_up, rw_a0, rw_a_up, rw_g_up, rw_k_k, rw_k_a, rw_r_k, rw_lnx_w, rw_lnx_b, w_out, ffn2_norm, ffn2_wg, ffn2_wu, ffn2_wd, ple_norm, ple_gate, ple_up, final_norm):
    w = dict(ffn1_norm=ffn1_norm, ffn1_wg=ffn1_wg, ffn1_wu=ffn1_wu, ffn1_wd=ffn1_wd, mix_norm=mix_norm,
             w_in=w_in, na_rpb=na_rpb, rw_mu=rw_mu, rw_w0=rw_w0, rw_w_up=rw_w_up, rw_a0=rw_a0,
             rw_a_up=rw_a_up, rw_g_up=rw_g_up, rw_k_k=rw_k_k, rw_k_a=rw_k_a, rw_r_k=rw_r_k,
             rw_lnx_w=rw_lnx_w, rw_lnx_b=rw_lnx_b, w_out=w_out, ffn2_norm=ffn2_norm, ffn2_wg=ffn2_wg,
             ffn2_wu=ffn2_wu, ffn2_wd=ffn2_wd, ple_norm=ple_norm, ple_gate=ple_gate, ple_up=ple_up)
    layers = [_layer_weights(w, i) for i in range(ffn1_wg.shape[0])]
    fn = final_norm.reshape(1, -1)
    return (_trunk(x_prompt, p_prompt, layers, fn), _trunk(x_sample, p_sample, layers, fn))
```

```python
import functools

import jax
import jax.numpy as jnp
from jax import lax
from jax.experimental import pallas as pl
from jax.experimental.pallas import tpu as pltpu

F32 = jnp.float32
BF16 = jnp.bfloat16

GRID_W = 64
HEAD_DIM = 64
NA_HEADS = 8
RW_HEADS = 8
NA_WIDTH = NA_HEADS * HEAD_DIM
RW_WIDTH = RW_HEADS * HEAD_DIM
WIN_ROWS = 8
WIN_COLS = 16
DECAY_LORA = 64
AAA_LORA = 64
GATE_LORA = 128
RW_COLS = 3 * RW_WIDTH + 2 * DECAY_LORA + 2 * AAA_LORA + GATE_LORA
NORM_EPS = 1e-6
LNX_EPS = 64e-5
DECAY_SCALE = 0.606531
MASK_VALUE = -1e30

LANES = 128
MXU_DIM = 256
VMEM_LIMIT_BYTES = 52 * 1024 * 1024

ROW_TILE = 512
FFN_SUBTILES = 2
NA_ROWS = 8
NA_UNROLL = 4
CHUNK = 64
WKV_STEP = 512
GROUP_HEADS = MXU_DIM // HEAD_DIM


def _params(*sem):
    return pltpu.CompilerParams(dimension_semantics=sem, vmem_limit_bytes=VMEM_LIMIT_BYTES)


def _bdot(a, b):
    return jnp.dot(a.astype(BF16), b.astype(BF16), preferred_element_type=F32)


def _split(x, n):
    parts = []
    rem = x
    for _ in range(n):
        p = rem.astype(BF16)
        parts.append(p)
        rem = rem - p.astype(F32)
    return parts


def _rms(x, g):
    ms = jnp.mean(x * x, axis=-1, keepdims=True)
    return x * lax.rsqrt(ms + NORM_EPS) * g


def _sigmoid(x):
    return 1.0 / (1.0 + jnp.exp(-x))


def _ffn_kernel(*refs, with_ple, with_final):
    if with_ple:
        x_ref, g_ref, wg_ref, wu_ref, wd_ref, p_ref, pn_ref, pg_ref, pu_ref, fn_ref, o_ref = refs
    else:
        x_ref, g_ref, wg_ref, wu_ref, wd_ref, o_ref = refs
    sub = x_ref.shape[0] // FFN_SUBTILES
    rows = [slice(i * sub, (i + 1) * sub) for i in range(FFN_SUBTILES)]
    xns = [_rms(x_ref[r], g_ref[...]).astype(BF16) for r in rows]
    gates = [jnp.dot(xn, wg_ref[...], preferred_element_type=F32) for xn in xns]
    ups = [jnp.dot(xn, wu_ref[...], preferred_element_type=F32) for xn in xns]
    mids = [((a * _sigmoid(a)) * b).astype(BF16) for a, b in zip(gates, ups)]
    downs = [jnp.dot(m, wd_ref[...], preferred_element_type=F32) for m in mids]
    hs = [x_ref[r] + 0.5 * d for r, d in zip(rows, downs)]
    if with_ple:
        pgate = [_sigmoid(_bdot(_rms(h, pn_ref[...]), pg_ref[...])) for h in hs]
        pup = [_bdot(p_ref[r], pu_ref[...]) for r in rows]
        hs = [h + gt * up for h, gt, up in zip(hs, pgate, pup)]
        if with_final:
            hs = [_rms(h, fn_ref[...]) for h in hs]
    for r, h in zip(rows, hs):
        o_ref[r] = h


def _ffn(x, g, wg, wu, wd, ple=None, final_norm=None):
    m, d = x.shape
    dff = wg.shape[1]
    tm = min(ROW_TILE, m)
    row = lambda i: (i, 0)
    const = lambda i: (0, 0)
    resident = lambda shape: pl.BlockSpec(shape, const, pipeline_mode=pl.Buffered(1))
    in_specs = [pl.BlockSpec((tm, d), row), pl.BlockSpec((1, d), const),
                resident((d, dff)), resident((d, dff)), resident((dff, d))]
    args = [x, g, wg, wu, wd]
    if ple is not None:
        p, pn, pg, pu = ple
        fn = final_norm if final_norm is not None else pn
        in_specs += [pl.BlockSpec((tm, p.shape[1]), row), pl.BlockSpec((1, d), const),
                     resident((d, d)), resident((p.shape[1], d)), pl.BlockSpec((1, d), const)]
        args += [p, pn, pg, pu, fn]
    kern = functools.partial(_ffn_kernel, with_ple=ple is not None, with_final=final_norm is not None)
    return pl.pallas_call(
        kern,
        grid=(m // tm,),
        in_specs=in_specs,
        out_specs=pl.BlockSpec((tm, d), row),
        out_shape=jax.ShapeDtypeStruct((m, d), F32),
        compiler_params=_params("parallel"),
        name="ffn",
    )(*args)


def _inproj_kernel(x_ref, xp_ref, xn_ref, g_ref, wq_ref, wr_ref, mu_ref, qkv_ref, um_ref, *, seq_len):
    i = pl.program_id(0)
    tm = x_ref.shape[0]
    half = tm // 2
    g = g_ref[...]
    lane = lax.broadcasted_iota(jnp.int32, (1, 3 * NA_WIDTH), 1)
    scale = jnp.where(lane < NA_WIDTH, HEAD_DIM ** -0.5, 1.0)
    has_prev = (i * tm) % seq_len != 0
    has_next = ((i + 1) * tm) % seq_len != 0
    xn = [_rms(x_ref[0:half], g).astype(BF16), _rms(x_ref[half:tm], g).astype(BF16)]
    before = [_rms(xp_ref[...], g).astype(BF16), _rms(x_ref[half - 8:half], g).astype(BF16)]
    after = [_rms(x_ref[half:half + 8], g).astype(BF16), _rms(xn_ref[...], g).astype(BF16)]
    keep_before = [has_prev, True]
    keep_after = [True, has_next]
    for s in range(2):
        rows = slice(s * half, (s + 1) * half)
        proj = jnp.dot(xn[s], wq_ref[...], preferred_element_type=F32)
        qkv_ref[rows] = (proj * scale).astype(BF16)
        ext = jnp.concatenate([before[s], xn[s], after[s]], axis=0)
        u = jnp.dot(ext, wr_ref[...], preferred_element_type=F32)
        u = jnp.concatenate([jnp.where(keep_before[s], u[0:8], 0.0), u[8:half + 8],
                             jnp.where(keep_after[s], u[half + 8:], 0.0)], axis=0)
        nbr = pltpu.roll(u, 1, axis=0) + pltpu.roll(u, half + 15, axis=0)
        uc = u[8:half + 8]
        um_ref[rows] = uc + mu_ref[...] * (0.5 * nbr[8:half + 8] - uc)


def _inproj(x, g, w_qkv, w_rw, mu, seq_len):
    m, d = x.shape
    tm = min(ROW_TILE, m, seq_len)
    na3 = 3 * NA_WIDTH
    hb = tm // 8
    nhb = m // 8
    const = lambda i: (0, 0)
    return pl.pallas_call(
        functools.partial(_inproj_kernel, seq_len=seq_len),
        grid=(m // tm,),
        in_specs=[
            pl.BlockSpec((tm, d), lambda i: (i, 0)),
            pl.BlockSpec((8, d), lambda i: (jnp.maximum(i * hb - 1, 0), 0)),
            pl.BlockSpec((8, d), lambda i: (jnp.minimum((i + 1) * hb, nhb - 1), 0)),
            pl.BlockSpec((1, d), const),
            pl.BlockSpec((d, na3), const),
            pl.BlockSpec((d, RW_COLS), const),
            pl.BlockSpec((1, RW_COLS), const),
        ],
        out_specs=[
            pl.BlockSpec((tm, na3), lambda i: (i, 0)),
            pl.BlockSpec((tm, RW_COLS), lambda i: (i, 0)),
        ],
        out_shape=[jax.ShapeDtypeStruct((m, na3), BF16), jax.ShapeDtypeStruct((m, RW_COLS), F32)],
        compiler_params=_params("parallel"),
        name="inproj",
    )(x, x, x, g, w_qkv, w_rw, mu)


def _na_bias_table(rpb):
    j = jnp.arange(GRID_W)
    c = jnp.arange(GRID_W)
    cs = jnp.clip(j - WIN_COLS // 2, 0, GRID_W - WIN_COLS)
    valid = (c[None, :] >= cs[:, None]) & (c[None, :] < cs[:, None] + WIN_COLS)
    cidx = jnp.clip(c[None, :] - j[:, None] + (WIN_COLS - 1), 0, 2 * WIN_COLS - 2)
    full = jnp.where(valid, rpb[:, :, cidx], MASK_VALUE)
    nri = 2 * WIN_ROWS - 2
    two = jnp.concatenate([full[:, 0:nri], full[:, 1:nri + 1]], axis=-1)
    two = two.reshape(NA_HEADS // 2, 2, nri, GRID_W, 2 * GRID_W)
    return jnp.transpose(two, (0, 2, 1, 3, 4)).reshape(NA_HEADS // 2, nri, 2 * GRID_W, 2 * GRID_W)


def _na_kernel(q_ref, kp_ref, kc_ref, kn_ref, vp_ref, vc_ref, vn_ref, tz_ref, o_ref,
               kbuf, vbuf, *, rows):
    qb = pl.program_id(1)
    blk = NA_ROWS * GRID_W
    kbuf[0:blk] = kp_ref[...]
    kbuf[blk:2 * blk] = kc_ref[...]
    kbuf[2 * blk:3 * blk] = kn_ref[...]
    vbuf[0:blk] = vp_ref[...]
    vbuf[blk:2 * blk] = vc_ref[...]
    vbuf[2 * blk:3 * blk] = vn_ref[...]
    lane = lax.broadcasted_iota(jnp.int32, (GRID_W, 2 * HEAD_DIM), 1)
    low = lane < HEAD_DIM
    win = WIN_ROWS * GRID_W

    npair = NA_HEADS // 2
    sls = [slice(hp * 2 * HEAD_DIM, (hp + 1) * 2 * HEAD_DIM) for hp in range(npair)]
    zero = jnp.zeros((GRID_W, 2 * HEAD_DIM), BF16)

    def body(it, carry):
        qoff, koff, delta = [], [], []
        for n in range(NA_UNROLL):
            il = it * NA_UNROLL + n
            i = qb * NA_ROWS + il
            rs = jnp.clip(i - WIN_ROWS // 2, 0, rows - WIN_ROWS)
            delta.append(i - rs)
            koff.append(pl.multiple_of((rs - qb * NA_ROWS + NA_ROWS) * GRID_W, GRID_W))
            qoff.append(pl.multiple_of(il * GRID_W, GRID_W))
        chains = [(n, hp) for n in range(NA_UNROLL) for hp in range(npair)]
        qps = [q_ref[pl.ds(qoff[n], GRID_W), sls[hp]] for n, hp in chains]
        qsts = [jnp.concatenate([jnp.where(low, qp, zero), jnp.where(low, zero, qp)], axis=0)
                for qp in qps]
        ss = [lax.dot_general(qsts[ci], kbuf[pl.ds(koff[n], win), sls[hp]], (((1,), (1,)), ((), ())),
                              preferred_element_type=F32)
              for ci, (n, hp) in enumerate(chains)]
        ss = [ss[ci] + jnp.concatenate(
            [tz_ref[hp, 2 * p - delta[n] + (WIN_ROWS - 1)] for p in range(WIN_ROWS // 2)], axis=1)
            for ci, (n, hp) in enumerate(chains)]
        ps = [jnp.exp(s - jnp.max(s, axis=1, keepdims=True)) for s in ss]
        ps = [p * (1.0 / jnp.sum(p, axis=1, keepdims=True)) for p in ps]
        os_ = [jnp.dot(ps[ci].astype(BF16), vbuf[pl.ds(koff[n], win), sls[hp]],
                       preferred_element_type=F32) for ci, (n, hp) in enumerate(chains)]
        outs = [jnp.where(low, o[0:GRID_W], o[GRID_W:2 * GRID_W]) for o in os_]
        for n in range(NA_UNROLL):
            o_ref[pl.ds(qoff[n], GRID_W), :] = jnp.concatenate(
                outs[n * npair:(n + 1) * npair], axis=1).astype(BF16)
        return carry

    lax.fori_loop(0, NA_ROWS // NA_UNROLL, body, 0)


def _na(qkv, tz):
    b, t, _ = qkv.shape
    rows = t // GRID_W
    nqb = rows // NA_ROWS
    blk = NA_ROWS * GRID_W
    spec = lambda col, shift: pl.BlockSpec(
        (None, blk, NA_WIDTH),
        lambda bi, qi: (bi, jnp.clip(qi + shift, 0, nqb - 1), col))
    return pl.pallas_call(
        functools.partial(_na_kernel, rows=rows),
        grid=(b, nqb),
        in_specs=[spec(0, 0), spec(1, -1), spec(1, 0), spec(1, 1), spec(2, -1), spec(2, 0), spec(2, 1),
                  pl.BlockSpec(tz.shape, lambda bi, qi: (0, 0, 0, 0))],
        out_specs=pl.BlockSpec((None, blk, NA_WIDTH), lambda bi, qi: (bi, qi, 0)),
        out_shape=jax.ShapeDtypeStruct((b, t, NA_WIDTH), BF16),
        scratch_shapes=[pltpu.VMEM((3 * blk, NA_WIDTH), BF16), pltpu.VMEM((3 * blk, NA_WIDTH), BF16)],
        compiler_params=_params("parallel", "arbitrary"),
        name="natten",
    )(qkv, qkv, qkv, qkv, qkv, qkv, qkv, tz)


def _head_sum(x, hm):
    outs = []
    for q in range(RW_WIDTH // MXU_DIM):
        xs = x[:, q * MXU_DIM:(q + 1) * MXU_DIM]
        outs.append(sum(jnp.dot(p, hm, preferred_element_type=F32) for p in _split(xs, 2)))
    return jnp.concatenate(outs, axis=1)


def _stack(x, low):
    xb = x.astype(BF16)
    zero = jnp.zeros((x.shape[0], LANES), BF16)
    blocks = []
    for h in range(GROUP_HEADS):
        tile = xb[:, (h // 2) * LANES:(h // 2 + 1) * LANES]
        kept = jnp.where(low, tile, zero) if h % 2 == 0 else jnp.where(low, zero, tile)
        blocks.append(jnp.concatenate([kept, zero] if h < 2 else [zero, kept], axis=1))
    return jnp.concatenate(blocks, axis=0)


def _head_transpose_pair(x0, x1, low):
    xt = jnp.concatenate([x0, x1], axis=0).T
    b = [xt[h * HEAD_DIM:(h + 1) * HEAD_DIM] for h in range(GROUP_HEADS)]
    rolled = [pltpu.roll(bh, HEAD_DIM, axis=1) for bh in b]
    y0 = jnp.concatenate([jnp.where(low, b[0], rolled[1]), jnp.where(low, b[2], rolled[3])], axis=1)
    y1 = jnp.concatenate([jnp.where(low, rolled[0], b[1]), jnp.where(low, rolled[2], b[3])], axis=1)
    return y0, y1


def _wkv_kernel(r_ref, k_ref, v_ref, wl_ref, al_ref, w0_ref, wup_ref, a0_ref, aup_ref,
                kk_ref, ka_ref, hm_ref, y_ref, z_ref, rp_ref, yh_ref, ql_ref, *, reverse, nsteps):
    g = pl.program_id(0)
    ts = r_ref.shape[0]
    nchunk = ts // CHUNK
    ngroup = RW_WIDTH // MXU_DIM

    @pl.when(g == 0)
    def _():
        z_ref[...] = jnp.zeros_like(z_ref)
        rp_ref[...] = jnp.zeros_like(rp_ref)
        yh_ref[...] = jnp.zeros_like(yh_ref)
        ql_ref[...] = jnp.zeros_like(ql_ref)

    ti = lax.broadcasted_iota(jnp.int32, (CHUNK, CHUNK), 0)
    si = lax.broadcasted_iota(jnp.int32, (CHUNK, CHUNK), 1)
    tri = jnp.where((si >= ti) if reverse else (si <= ti), 1.0, 0.0).astype(BF16)
    lane = lax.broadcasted_iota(jnp.int32, (CHUNK, MXU_DIM), 1)
    trow = lax.broadcasted_iota(jnp.int32, (CHUNK, MXU_DIM), 0)
    s_of_lane = lane % HEAD_DIM
    if reverse:
        strict, incl = s_of_lane > trow, s_of_lane >= trow
    else:
        strict, incl = s_of_lane < trow, s_of_lane <= trow
    eye_lc = jnp.where(s_of_lane == trow, 1.0, 0.0)
    low = lax.broadcasted_iota(jnp.int32, (CHUNK, LANES), 1) < HEAD_DIM

    chains = [(c, q) for c in range(nchunk) for q in range(ngroup)]
    lanes = [slice(q * MXU_DIM, (q + 1) * MXU_DIM) for q in range(ngroup)]
    rows = [slice(c * CHUNK, (c + 1) * CHUNK) for c in range(nchunk)]
    st = lambda x: _stack(x, low)

    order = list(range(nchunk - 1, -1, -1) if reverse else range(nchunk))
    starts_sequence = (g + nsteps - 1) % nsteps == 0
    zs = [jnp.where(starts_sequence, 0.0, z_ref[q]) for q in range(ngroup)]
    pending = list(order)

    def scan_one_chunk():
        if not pending:
            return
        c = pending.pop(0)
        ys = []
        for q in range(ngroup):
            both = jnp.dot(rp_ref[c, q], st(zs[q]), preferred_element_type=F32)
            ys.append(both[0:CHUNK] + yh_ref[c, q])
            zs[q] = both[CHUNK:] + ql_ref[c, q]
        y_ref[c * CHUNK:(c + 1) * CHUNK, :] = jnp.concatenate(ys, axis=1)

    scan_one_chunk()
    r = r_ref[...]
    k = k_ref[...]
    v = v_ref[...]
    logw = -DECAY_SCALE * _sigmoid(w0_ref[...] + _bdot(jnp.tanh(wl_ref[...]), wup_ref[...]))
    a = _sigmoid(a0_ref[...] + _bdot(al_ref[...], aup_ref[...]))
    scan_one_chunk()
    kk = k * kk_ref[...]
    kk = kk / jnp.maximum(jnp.sqrt(_head_sum(kk * kk, hm_ref[...])), 1e-12)
    kd = k * (1.0 + (a - 1.0) * ka_ref[...])
    bb = kk * a
    scan_one_chunk()
    cums = [sum(jnp.dot(tri, p, preferred_element_type=F32) for p in _split(logw[rows[c]], 3))
            for c in range(nchunk)]
    scan_one_chunk()
    dec = []
    for c in range(nchunk):
        cum, lw = cums[c], logw[rows[c]]
        cmid = cum[CHUNK // 2:CHUNK // 2 + 1]
        ctot = cum[0:1] if reverse else cum[CHUNK - 1:CHUNK]
        e1 = jnp.exp(cum - cmid)
        e1i = jnp.exp(cmid - cum)
        ee = jnp.exp(ctot - cum)
        emid = jnp.exp(cmid)
        r_g = r[rows[c]] * e1
        a_g = -kk[rows[c]] * (e1 * jnp.exp(-lw))
        dec.append(dict(
            r_g=r_g, a_g=a_g, b_g=bb[rows[c]] * e1i, k_g=kd[rows[c]] * e1i, r_s=r_g * emid,
            a_s=a_g * emid, b_e=bb[rows[c]] * ee, k_e=kd[rows[c]] * ee, v=v[rows[c]],
            wtot=jnp.exp(ctot)))

    tr = {}
    for name in ("b_g", "k_g", "b_e", "k_e"):
        for c in range(0, nchunk, 2):
            for q in range(ngroup):
                tr[name, c, q], tr[name, c + 1, q] = _head_transpose_pair(
                    dec[c][name][:, lanes[q]], dec[c + 1][name][:, lanes[q]], low)

    st_v = [st(dec[c]["v"][:, lanes[q]]) for c, q in chains]
    grams = [_bdot(jnp.concatenate([dec[c]["a_g"][:, lanes[q]], dec[c]["r_g"][:, lanes[q]]], axis=0),
                   jnp.concatenate([st(tr["b_g", c, q]), st(tr["k_g", c, q])], axis=1))
             for c, q in chains]
    scan_one_chunk()
    a_ab = [jnp.where(strict, g_[0:CHUNK, 0:MXU_DIM], 0.0) for g_ in grams]
    a_kr = [jnp.concatenate([jnp.where(strict, g_[0:CHUNK, MXU_DIM:], 0.0),
                             jnp.where(incl, g_[CHUNK:, MXU_DIM:], 0.0)], axis=0) for g_ in grams]
    a_rb = [jnp.where(incl, g_[CHUNK:, 0:MXU_DIM], 0.0) for g_ in grams]
    tinv = [eye_lc + a_ for a_ in a_ab]
    apow = [_bdot(a_, st(a_)) for a_ in a_ab]
    scan_one_chunk()
    npow = CHUNK.bit_length() - 2
    for jj in range(npow):
        if jj < npow - 1:
            both = [_bdot(jnp.concatenate([ap, ti_], axis=0), st(ap)) for ap, ti_ in zip(apow, tinv)]
            apow = [b_[0:CHUNK] for b_ in both]
            tinv = [ti_ + b_[CHUNK:] for ti_, b_ in zip(tinv, both)]
        else:
            tinv = [ti_ + _bdot(ti_, st(ap)) for ap, ti_ in zip(apow, tinv)]
        scan_one_chunk()
    xyq = [_bdot(jnp.concatenate([a_kr[i], tr["k_e", c, q]], axis=0), st_v[i])
           for i, (c, q) in enumerate(chains)]
    xy = [x[0:2 * CHUNK] for x in xyq]
    q2 = [x[2 * CHUNK:] for x in xyq]
    au = [_bdot(tinv[i], jnp.concatenate([st(dec[c]["a_s"][:, lanes[q]]), st(xy[i][0:CHUNK])], axis=1))
          for i, (c, q) in enumerate(chains)]
    while pending:
        scan_one_chunk()
    st_au = [jnp.concatenate([st(x[:, 0:MXU_DIM]), st(x[:, MXU_DIM:])], axis=1) for x in au]
    rypq = [_bdot(jnp.concatenate([a_rb[i], tr["b_e", c, q]], axis=0), st_au[i])
            for i, (c, q) in enumerate(chains)]
    ry = [x[0:CHUNK] for x in rypq]
    pq = [x[CHUNK:] for x in rypq]
    for q in range(ngroup):
        z_ref[q] = zs[q]
    for i, (c, q) in enumerate(chains):
        r_hat = dec[c]["r_s"][:, lanes[q]] + ry[i][:, 0:MXU_DIM]
        p_lc = eye_lc * dec[c]["wtot"][:, lanes[q]] + pq[i][:, 0:MXU_DIM]
        rp_ref[c, q] = jnp.concatenate([r_hat, p_lc], axis=0).astype(BF16)
        yh_ref[c, q] = ry[i][:, MXU_DIM:] + xy[i][CHUNK:]
        ql_ref[c, q] = pq[i][:, MXU_DIM:] + q2[i]


_UM_WL_BLOCK = 3 * RW_WIDTH // LANES
_UM_AL_BLOCK = _UM_WL_BLOCK + 1
_UM_GL_BLOCK = _UM_WL_BLOCK + 2


def _wkv(um, wp, reverse):
    b, t, _ = um.shape
    ts = min(WKV_STEP, t)
    nsteps = t // ts
    nblocks = b * nsteps
    nchunk = ts // CHUNK
    ngroup = RW_WIDTH // MXU_DIM
    pos = (lambda si: nsteps - 1 - si) if reverse else (lambda si: si)

    def block_of(gi, col):
        return (gi // nsteps, pos(gi % nsteps), col)

    const = lambda g: (0, 0)
    vec = pl.BlockSpec((1, RW_WIDTH), const)
    wide = lambda col: pl.BlockSpec((None, ts, RW_WIDTH),
                                    lambda g: block_of(jnp.minimum(g, nblocks - 1), col))
    narrow = lambda col: pl.BlockSpec((None, ts, LANES),
                                      lambda g: block_of(jnp.minimum(g, nblocks - 1), col))
    return pl.pallas_call(
        functools.partial(_wkv_kernel, reverse=reverse, nsteps=nsteps),
        grid=(nblocks + 1,),
        in_specs=[
            wide(0), wide(1), wide(2), narrow(_UM_WL_BLOCK), narrow(_UM_AL_BLOCK),
            vec,
            pl.BlockSpec((2 * DECAY_LORA, RW_WIDTH), const),
            vec,
            pl.BlockSpec((2 * AAA_LORA, RW_WIDTH), const),
            vec, vec,
            pl.BlockSpec((MXU_DIM, MXU_DIM), const),
        ],
        out_specs=pl.BlockSpec((None, ts, RW_WIDTH), lambda g: block_of(jnp.maximum(g - 1, 0), 0)),
        out_shape=jax.ShapeDtypeStruct((b, t, RW_WIDTH), F32),
        scratch_shapes=[pltpu.VMEM((ngroup, HEAD_DIM, MXU_DIM), F32),
                        pltpu.VMEM((nchunk, ngroup, 2 * CHUNK, MXU_DIM), BF16),
                        pltpu.VMEM((nchunk, ngroup, CHUNK, MXU_DIM), F32),
                        pltpu.VMEM((nchunk, ngroup, HEAD_DIM, MXU_DIM), F32)],
        compiler_params=_params("arbitrary"),
        name="wkv_bwd" if reverse else "wkv_fwd",
    )(um, um, um, um, um, wp["w0"], wp["wup"], wp["a0"], wp["aup"], wp["k_k"], wp["k_a"], wp["hm"])


def _outproj_kernel(h_ref, yna_ref, y0_ref, y1_ref, r_ref, k_ref, v_ref, al_ref, gl_ref,
                    a00_ref, aup0_ref, a01_ref, aup1_ref, ka_ref, rk_ref, gup_ref, lw_ref, lb_ref,
                    hm_ref, w_ref, o_ref):
    hm = hm_ref[...]
    half = h_ref.shape[0] // 2
    groups = [slice(0, half), slice(half, 2 * half)]
    wkv = [y0_ref[rows] + y1_ref[rows] for rows in groups]
    mean = [_head_sum(x, hm) * (1.0 / HEAD_DIM) for x in wkv]
    dev = [x - m_ for x, m_ in zip(wkv, mean)]
    var = [_head_sum(d_ * d_, hm) * (1.0 / HEAD_DIM) for d_ in dev]
    yn = [d_ * lax.rsqrt(v_ + LNX_EPS) * lw_ref[...] + lb_ref[...] for d_, v_ in zip(dev, var)]
    a_sum = [_sigmoid(a00_ref[...] + _bdot(al_ref[rows], aup0_ref[...]))
             + _sigmoid(a01_ref[...] + _bdot(al_ref[rows], aup1_ref[...])) for rows in groups]
    kd_sum = [k_ref[rows] * (2.0 + (a_ - 2.0) * ka_ref[...]) for rows, a_ in zip(groups, a_sum)]
    bonus = [_head_sum(r_ref[rows] * kd * rk_ref[...], hm) * v_ref[rows] for rows, kd in zip(groups, kd_sum)]
    gate = [_bdot(_sigmoid(gl_ref[rows]), gup_ref[...]) for rows in groups]
    for rows, yn_, bonus_, gate_ in zip(groups, yn, bonus, gate):
        y_rw = (yn_ + bonus_) * gate_
        mix = jnp.concatenate([yna_ref[rows], y_rw.astype(BF16)], axis=1)
        o_ref[rows] = h_ref[rows] + jnp.dot(mix, w_ref[...], preferred_element_type=F32)


def _outproj(h, yna, y0, y1, um, lw):
    m, d = h.shape
    tm = min(ROW_TILE, m)
    row = lambda i: (i, 0)
    const = lambda i: (0, 0)
    tok = pl.BlockSpec((tm, RW_WIDTH), row)
    vec = pl.BlockSpec((1, RW_WIDTH), const)
    um_wide = lambda col: pl.BlockSpec((tm, RW_WIDTH), lambda i: (i, col))
    um_narrow = lambda col: pl.BlockSpec((tm, LANES), lambda i: (i, col))
    lora = pl.BlockSpec((2 * AAA_LORA, RW_WIDTH), const)
    d0, d1 = lw["dirs"]
    return pl.pallas_call(
        _outproj_kernel,
        grid=(m // tm,),
        in_specs=[pl.BlockSpec((tm, d), row), pl.BlockSpec((tm, NA_WIDTH), row), tok, tok,
                  um_wide(0), um_wide(1), um_wide(2), um_narrow(_UM_AL_BLOCK), um_narrow(_UM_GL_BLOCK),
                  vec, lora, vec, lora, vec, vec, pl.BlockSpec((GATE_LORA, RW_WIDTH), const), vec, vec,
                  pl.BlockSpec((MXU_DIM, MXU_DIM), const),
                  pl.BlockSpec((NA_WIDTH + RW_WIDTH, d), const)],
        out_specs=pl.BlockSpec((tm, d), row),
        out_shape=jax.ShapeDtypeStruct((m, d), F32),
        compiler_params=_params("parallel"),
        name="outproj",
    )(h, yna, y0, y1, um, um, um, um, um, d0["a0"], d0["aup"], d1["a0"], d1["aup"], d0["k_a"],
      lw["r_k"], lw["gup"], lw["lnx_w"], lw["lnx_b"], lw["hm"], lw["w_out"])


def _pad_lora(w_up):
    zero = jnp.zeros_like(w_up[0])
    return (jnp.concatenate([w_up[0], zero], axis=0).astype(BF16),
            jnp.concatenate([zero, w_up[1]], axis=0).astype(BF16))


def _layer_weights(w, i):
    row = lambda x: x.reshape(1, -1)
    wup = _pad_lora(w["rw_w_up"][i])
    aup = _pad_lora(w["rw_a_up"][i])
    lane = jnp.arange(MXU_DIM)
    hm = (lane[:, None] // HEAD_DIM == lane[None, :] // HEAD_DIM).astype(BF16)
    common = dict(k_k=row(w["rw_k_k"][i]), k_a=row(w["rw_k_a"][i]), hm=hm)
    dirs = [dict(common, w0=row(w["rw_w0"][i, d]), wup=wup[d], a0=row(w["rw_a0"][i, d]), aup=aup[d])
            for d in range(2)]
    w_in = w["w_in"][i].astype(BF16)
    return dict(
        ffn1=(row(w["ffn1_norm"][i]), w["ffn1_wg"][i].astype(BF16), w["ffn1_wu"][i].astype(BF16),
              w["ffn1_wd"][i].astype(BF16)),
        ffn2=(row(w["ffn2_norm"][i]), w["ffn2_wg"][i].astype(BF16), w["ffn2_wu"][i].astype(BF16),
              w["ffn2_wd"][i].astype(BF16)),
        mix_norm=row(w["mix_norm"][i]), w_qkv=w_in[:, :3 * NA_WIDTH], w_rw=w_in[:, 3 * NA_WIDTH:],
        mu=row(w["rw_mu"][i]), tz=_na_bias_table(w["na_rpb"][i]), dirs=dirs, hm=hm,
        r_k=row(w["rw_r_k"][i]), gup=w["rw_g_up"][i].astype(BF16),
        lnx_w=row(w["rw_lnx_w"][i]), lnx_b=row(w["rw_lnx_b"][i]), w_out=w["w_out"][i].astype(BF16),
        ple=(row(w["ple_norm"][i]), w["ple_gate"][i].astype(BF16), w["ple_up"][i].astype(BF16)),
    )


def _trunk(x, p, layers, final_norm):
    b, t, d = x.shape
    m = b * t
    h = x.reshape(m, d)
    depth = len(layers)
    for i, lw in enumerate(layers):
        h = _ffn(h, *lw["ffn1"])
        qkv, um = _inproj(h, lw["mix_norm"], lw["w_qkv"], lw["w_rw"], lw["mu"], seq_len=t)
        yna = _na(qkv.reshape(b, t, -1), lw["tz"])
        um3 = um.reshape(b, t, -1)
        y0 = _wkv(um3, lw["dirs"][0], reverse=False)
        y1 = _wkv(um3, lw["dirs"][1], reverse=True)
        h = _outproj(h, yna.reshape(m, -1), y0.reshape(m, -1), y1.reshape(m, -1), um, lw)
        pn, pg, pu = lw["ple"]
        h = _ffn(h, *lw["ffn2"], ple=(p[i].reshape(m, -1), pn, pg, pu),
                 final_norm=final_norm if i == depth - 1 else None)
    return h.reshape(b, t, d)


def kernel(x_prompt, x_sample, p_prompt, p_sample, ffn1_norm, ffn1_wg, ffn1_wu, ffn1_wd, mix_norm, w_in, na_rpb, rw_mu, rw_w0, rw_w_up, rw_a0, rw_a_up, rw_g_up, rw_k_k, rw_k_a, rw_r_k, rw_lnx_w, rw_lnx_b, w_out, ffn2_norm, ffn2_wg, ffn2_wu, ffn2_wd, ple_norm, ple_gate, ple_up, final_norm):
    w = dict(ffn1_norm=ffn1_norm, ffn1_wg=ffn1_wg, ffn1_wu=ffn1_wu, ffn1_wd=ffn1_wd, mix_norm=mix_norm,
             w_in=w_in, na_rpb=na_rpb, rw_mu=rw_mu, rw_w0=rw_w0, rw_w_up=rw_w_up, rw_a0=rw_a0,
             rw_a_up=rw_a_up, rw_g_up=rw_g_up, rw_k_k=rw_k_k, rw_k_a=rw_k_a, rw_r_k=rw_r_k,
             rw_lnx_w=rw_lnx_w, rw_lnx_b=rw_lnx_b, w_out=w_out, ffn2_norm=ffn2_norm, ffn2_wg=ffn2_wg,
             ffn2_wu=ffn2_wu, ffn2_wd=ffn2_wd, ple_norm=ple_norm, ple_gate=ple_gate, ple_up=ple_up)
    layers = [_layer_weights(w, i) for i in range(ffn1_wg.shape[0])]
    fn = final_norm.reshape(1, -1)
    return (_trunk(x_prompt, p_prompt, layers, fn), _trunk(x_sample, p_sample, layers, fn))
```

```python
import functools

import jax
import jax.numpy as jnp
from jax import lax
from jax.experimental import pallas as pl
from jax.experimental.pallas import tpu as pltpu

F32 = jnp.float32
BF16 = jnp.bfloat16

GRID_W = 64
HEAD_DIM = 64
NA_HEADS = 8
RW_HEADS = 8
NA_WIDTH = NA_HEADS * HEAD_DIM
RW_WIDTH = RW_HEADS * HEAD_DIM
WIN_ROWS = 8
WIN_COLS = 16
DECAY_LORA = 64
AAA_LORA = 64
GATE_LORA = 128
RW_COLS = 3 * RW_WIDTH + 2 * DECAY_LORA + 2 * AAA_LORA + GATE_LORA
NORM_EPS = 1e-6
LNX_EPS = 64e-5
DECAY_SCALE = 0.606531
MASK_VALUE = -1e30

LANES = 128
MXU_DIM = 256
VMEM_LIMIT_BYTES = 52 * 1024 * 1024

ROW_TILE = 512
FFN_SUBTILES = 2
NA_ROWS = 8
NA_UNROLL = 4
CHUNK = 64
WKV_STEP = 512
GROUP_HEADS = MXU_DIM // HEAD_DIM


def _params(*sem):
    return pltpu.CompilerParams(dimension_semantics=sem, vmem_limit_bytes=VMEM_LIMIT_BYTES)


def _bdot(a, b):
    return jnp.dot(a.astype(BF16), b.astype(BF16), preferred_element_type=F32)


def _split(x, n):
    parts = []
    rem = x
    for _ in range(n):
        p = rem.astype(BF16)
        parts.append(p)
        rem = rem - p.astype(F32)
    return parts


def _rms(x, g):
    ms = jnp.mean(x * x, axis=-1, keepdims=True)
    return x * lax.rsqrt(ms + NORM_EPS) * g


def _sigmoid(x):
    return 1.0 / (1.0 + jnp.exp(-x))


def _ffn_kernel(*refs, with_ple, with_final):
    if with_ple:
        x_ref, g_ref, wg_ref, wu_ref, wd_ref, p_ref, pn_ref, pg_ref, pu_ref, fn_ref, o_ref = refs
    else:
        x_ref, g_ref, wg_ref, wu_ref, wd_ref, o_ref = refs
    sub = x_ref.shape[0] // FFN_SUBTILES
    rows = [slice(i * sub, (i + 1) * sub) for i in range(FFN_SUBTILES)]
    xns = [_rms(x_ref[r], g_ref[...]).astype(BF16) for r in rows]
    gates = [jnp.dot(xn, wg_ref[...], preferred_element_type=F32) for xn in xns]
    ups = [jnp.dot(xn, wu_ref[...], preferred_element_type=F32) for xn in xns]
    mids = [((a * _sigmoid(a)) * b).astype(BF16) for a, b in zip(gates, ups)]
    downs = [jnp.dot(m, wd_ref[...], preferred_element_type=F32) for m in mids]
    hs = [x_ref[r] + 0.5 * d for r, d in zip(rows, downs)]
    if with_ple:
        pgate = [_sigmoid(_bdot(_rms(h, pn_ref[...]), pg_ref[...])) for h in hs]
        pup = [_bdot(p_ref[r], pu_ref[...]) for r in rows]
        hs = [h + gt * up for h, gt, up in zip(hs, pgate, pup)]
        if with_final:
            hs = [_rms(h, fn_ref[...]) for h in hs]
    for r, h in zip(rows, hs):
        o_ref[r] = h


def _ffn(x, g, wg, wu, wd, ple=None, final_norm=None):
    m, d = x.shape
    dff = wg.shape[1]
    tm = min(ROW_TILE, m)
    row = lambda i: (i, 0)
    const = lambda i: (0, 0)
    resident = lambda shape: pl.BlockSpec(shape, const, pipeline_mode=pl.Buffered(1))
    in_specs = [pl.BlockSpec((tm, d), row), pl.BlockSpec((1, d), const),
                resident((d, dff)), resident((d, dff)), resident((dff, d))]
    args = [x, g, wg, wu, wd]
    if ple is not None:
        p, pn, pg, pu = ple
        fn = final_norm if final_norm is not None else pn
        in_specs += [pl.BlockSpec((tm, p.shape[1]), row), pl.BlockSpec((1, d), const),
                     resident((d, d)), resident((p.shape[1], d)), pl.BlockSpec((1, d), const)]
        args += [p, pn, pg, pu, fn]
    kern = functools.partial(_ffn_kernel, with_ple=ple is not None, with_final=final_norm is not None)
    return pl.pallas_call(
        kern,
        grid=(m // tm,),
        in_specs=in_specs,
        out_specs=pl.BlockSpec((tm, d), row),
        out_shape=jax.ShapeDtypeStruct((m, d), F32),
        compiler_params=_params("parallel"),
        name="ffn",
    )(*args)


def _inproj_kernel(x_ref, xp_ref, xn_ref, g_ref, wq_ref, wr_ref, mu_ref, qkv_ref, um_ref, *, seq_len):
    i = pl.program_id(0)
    tm = x_ref.shape[0]
    half = tm // 2
    g = g_ref[...]
    lane = lax.broadcasted_iota(jnp.int32, (1, 3 * NA_WIDTH), 1)
    scale = jnp.where(lane < NA_WIDTH, HEAD_DIM ** -0.5, 1.0)
    has_prev = (i * tm) % seq_len != 0
    has_next = ((i + 1) * tm) % seq_len != 0
    xn = [_rms(x_ref[0:half], g).astype(BF16), _rms(x_ref[half:tm], g).astype(BF16)]
    before = [_rms(xp_ref[...], g).astype(BF16), _rms(x_ref[half - 8:half], g).astype(BF16)]
    after = [_rms(x_ref[half:half + 8], g).astype(BF16), _rms(xn_ref[...], g).astype(BF16)]
    keep_before = [has_prev, True]
    keep_after = [True, has_next]
    for s in range(2):
        rows = slice(s * half, (s + 1) * half)
        proj = jnp.dot(xn[s], wq_ref[...], preferred_element_type=F32)
        qkv_ref[rows] = (proj * scale).astype(BF16)
        ext = jnp.concatenate([before[s], xn[s], after[s]], axis=0)
        u = jnp.dot(ext, wr_ref[...], preferred_element_type=F32)
        u = jnp.concatenate([jnp.where(keep_before[s], u[0:8], 0.0), u[8:half + 8],
                             jnp.where(keep_after[s], u[half + 8:], 0.0)], axis=0)
        nbr = pltpu.roll(u, 1, axis=0) + pltpu.roll(u, half + 15, axis=0)
        uc = u[8:half + 8]
        um_ref[rows] = uc + mu_ref[...] * (0.5 * nbr[8:half + 8] - uc)


def _inproj(x, g, w_qkv, w_rw, mu, seq_len):
    m, d = x.shape
    tm = min(ROW_TILE, m, seq_len)
    na3 = 3 * NA_WIDTH
    hb = tm // 8
    nhb = m // 8
    const = lambda i: (0, 0)
    return pl.pallas_call(
        functools.partial(_inproj_kernel, seq_len=seq_len),
        grid=(m // tm,),
        in_specs=[
            pl.BlockSpec((tm, d), lambda i: (i, 0)),
            pl.BlockSpec((8, d), lambda i: (jnp.maximum(i * hb - 1, 0), 0)),
            pl.BlockSpec((8, d), lambda i: (jnp.minimum((i + 1) * hb, nhb - 1), 0)),
            pl.BlockSpec((1, d), const),
            pl.BlockSpec((d, na3), const),
            pl.BlockSpec((d, RW_COLS), const),
            pl.BlockSpec((1, RW_COLS), const),
        ],
        out_specs=[
            pl.BlockSpec((tm, na3), lambda i: (i, 0)),
            pl.BlockSpec((tm, RW_COLS), lambda i: (i, 0)),
        ],
        out_shape=[jax.ShapeDtypeStruct((m, na3), BF16), jax.ShapeDtypeStruct((m, RW_COLS), F32)],
        compiler_params=_params("parallel"),
        name="inproj",
    )(x, x, x, g, w_qkv, w_rw, mu)


def _na_bias_table(rpb):
    j = jnp.arange(GRID_W)
    c = jnp.arange(GRID_W)
    cs = jnp.clip(j - WIN_COLS // 2, 0, GRID_W - WIN_COLS)
    valid = (c[None, :] >= cs[:, None]) & (c[None, :] < cs[:, None] + WIN_COLS)
    cidx = jnp.clip(c[None, :] - j[:, None] + (WIN_COLS - 1), 0, 2 * WIN_COLS - 2)
    full = jnp.where(valid, rpb[:, :, cidx], MASK_VALUE)
    nri = 2 * WIN_ROWS - 2
    two = jnp.concatenate([full[:, 0:nri], full[:, 1:nri + 1]], axis=-1)
    two = two.reshape(NA_HEADS // 2, 2, nri, GRID_W, 2 * GRID_W)
    return jnp.transpose(two, (0, 2, 1, 3, 4)).reshape(NA_HEADS // 2, nri, 2 * GRID_W, 2 * GRID_W)


def _na_kernel(q_ref, kp_ref, kc_ref, kn_ref, vp_ref, vc_ref, vn_ref, tz_ref, o_ref,
               kbuf, vbuf, *, rows):
    qb = pl.program_id(1)
    blk = NA_ROWS * GRID_W
    kbuf[0:blk] = kp_ref[...]
    kbuf[blk:2 * blk] = kc_ref[...]
    kbuf[2 * blk:3 * blk] = kn_ref[...]
    vbuf[0:blk] = vp_ref[...]
    vbuf[blk:2 * blk] = vc_ref[...]
    vbuf[2 * blk:3 * blk] = vn_ref[...]
    lane = lax.broadcasted_iota(jnp.int32, (GRID_W, 2 * HEAD_DIM), 1)
    low = lane < HEAD_DIM
    win = WIN_ROWS * GRID_W

    npair = NA_HEADS // 2
    sls = [slice(hp * 2 * HEAD_DIM, (hp + 1) * 2 * HEAD_DIM) for hp in range(npair)]
    zero = jnp.zeros((GRID_W, 2 * HEAD_DIM), BF16)

    def body(it, carry):
        qoff, koff, delta = [], [], []
        for n in range(NA_UNROLL):
            il = it * NA_UNROLL + n
            i = qb * NA_ROWS + il
            rs = jnp.clip(i - WIN_ROWS // 2, 0, rows - WIN_ROWS)
            delta.append(i - rs)
            koff.append(pl.multiple_of((rs - qb * NA_ROWS + NA_ROWS) * GRID_W, GRID_W))
            qoff.append(pl.multiple_of(il * GRID_W, GRID_W))
        chains = [(n, hp) for n in range(NA_UNROLL) for hp in range(npair)]
        qps = [q_ref[pl.ds(qoff[n], GRID_W), sls[hp]] for n, hp in chains]
        qsts = [jnp.concatenate([jnp.where(low, qp, zero), jnp.where(low, zero, qp)], axis=0)
                for qp in qps]
        ss = [lax.dot_general(qsts[ci], kbuf[pl.ds(koff[n], win), sls[hp]], (((1,), (1,)), ((), ())),
                              preferred_element_type=F32)
              for ci, (n, hp) in enumerate(chains)]
        ss = [ss[ci] + jnp.concatenate(
            [tz_ref[hp, 2 * p - delta[n] + (WIN_ROWS - 1)] for p in range(WIN_ROWS // 2)], axis=1)
            for ci, (n, hp) in enumerate(chains)]
        ps = [jnp.exp(s - jnp.max(s, axis=1, keepdims=True)) for s in ss]
        ps = [p * (1.0 / jnp.sum(p, axis=1, keepdims=True)) for p in ps]
        os_ = [jnp.dot(ps[ci].astype(BF16), vbuf[pl.ds(koff[n], win), sls[hp]],
                       preferred_element_type=F32) for ci, (n, hp) in enumerate(chains)]
        outs = [jnp.where(low, o[0:GRID_W], o[GRID_W:2 * GRID_W]) for o in os_]
        for n in range(NA_UNROLL):
            o_ref[pl.ds(qoff[n], GRID_W), :] = jnp.concatenate(
                outs[n * npair:(n + 1) * npair], axis=1).astype(BF16)
        return carry

    lax.fori_loop(0, NA_ROWS // NA_UNROLL, body, 0)


def _na(qkv, tz):
    b, t, _ = qkv.shape
    rows = t // GRID_W
    nqb = rows // NA_ROWS
    blk = NA_ROWS * GRID_W
    spec = lambda col, shift: pl.BlockSpec(
        (None, blk, NA_WIDTH),
        lambda bi, qi: (bi, jnp.clip(qi + shift, 0, nqb - 1), col))
    return pl.pallas_call(
        functools.partial(_na_kernel, rows=rows),
        grid=(b, nqb),
        in_specs=[spec(0, 0), spec(1, -1), spec(1, 0), spec(1, 1), spec(2, -1), spec(2, 0), spec(2, 1),
                  pl.BlockSpec(tz.shape, lambda bi, qi: (0, 0, 0, 0))],
        out_specs=pl.BlockSpec((None, blk, NA_WIDTH), lambda bi, qi: (bi, qi, 0)),
        out_shape=jax.ShapeDtypeStruct((b, t, NA_WIDTH), BF16),
        scratch_shapes=[pltpu.VMEM((3 * blk, NA_WIDTH), BF16), pltpu.VMEM((3 * blk, NA_WIDTH), BF16)],
        compiler_params=_params("parallel", "arbitrary"),
        name="natten",
    )(qkv, qkv, qkv, qkv, qkv, qkv, qkv, tz)


def _head_sum(x, hm):
    return jnp.concatenate([_bdot(x[:, q * MXU_DIM:(q + 1) * MXU_DIM], hm)
                            for q in range(RW_WIDTH // MXU_DIM)], axis=1)


def _stack(x, low):
    xb = x.astype(BF16)
    zero = jnp.zeros((x.shape[0], LANES), BF16)
    blocks = []
    for h in range(GROUP_HEADS):
        tile = xb[:, (h // 2) * LANES:(h // 2 + 1) * LANES]
        kept = jnp.where(low, tile, zero) if h % 2 == 0 else jnp.where(low, zero, tile)
        blocks.append(jnp.concatenate([kept, zero] if h < 2 else [zero, kept], axis=1))
    return jnp.concatenate(blocks, axis=0)


def _head_transpose_pair(x0, x1, low):
    xt = jnp.concatenate([x0, x1], axis=0).T
    b = [xt[h * HEAD_DIM:(h + 1) * HEAD_DIM] for h in range(GROUP_HEADS)]
    rolled = [pltpu.roll(bh, HEAD_DIM, axis=1) for bh in b]
    y0 = jnp.concatenate([jnp.where(low, b[0], rolled[1]), jnp.where(low, b[2], rolled[3])], axis=1)
    y1 = jnp.concatenate([jnp.where(low, rolled[0], b[1]), jnp.where(low, rolled[2], b[3])], axis=1)
    return y0, y1


def _wkv_program(g, r_ref, k_ref, v_ref, wl_ref, al_ref, w0_ref, wup_ref, a0_ref, aup_ref,
                 kk_ref, ka_ref, hm_ref, y_ref, z_ref, rp_ref, yh_ref, ql_ref, *, reverse, nsteps):
    ts = r_ref.shape[0]
    nchunk = ts // CHUNK
    ngroup = RW_WIDTH // MXU_DIM

    ti = lax.broadcasted_iota(jnp.int32, (CHUNK, CHUNK), 0)
    si = lax.broadcasted_iota(jnp.int32, (CHUNK, CHUNK), 1)
    tri = jnp.where((si >= ti) if reverse else (si <= ti), 1.0, 0.0).astype(BF16)
    lane = lax.broadcasted_iota(jnp.int32, (CHUNK, MXU_DIM), 1)
    trow = lax.broadcasted_iota(jnp.int32, (CHUNK, MXU_DIM), 0)
    s_of_lane = lane % HEAD_DIM
    if reverse:
        strict, incl = s_of_lane > trow, s_of_lane >= trow
    else:
        strict, incl = s_of_lane < trow, s_of_lane <= trow
    eye_lc = jnp.where(s_of_lane == trow, 1.0, 0.0)
    low = lax.broadcasted_iota(jnp.int32, (CHUNK, LANES), 1) < HEAD_DIM

    chains = [(c, q) for c in range(nchunk) for q in range(ngroup)]
    lanes = [slice(q * MXU_DIM, (q + 1) * MXU_DIM) for q in range(ngroup)]
    rows = [slice(c * CHUNK, (c + 1) * CHUNK) for c in range(nchunk)]
    st = lambda x: _stack(x, low)

    starts_sequence = (g + nsteps - 1) % nsteps == 0
    zs = [jnp.where(starts_sequence, 0.0, z_ref[q]) for q in range(ngroup)]
    pending = list(range(nchunk - 1, -1, -1) if reverse else range(nchunk))

    def scan_one_chunk():
        if not pending:
            return
        c = pending.pop(0)
        ys = []
        for q in range(ngroup):
            both = jnp.dot(rp_ref[c, q], st(zs[q]), preferred_element_type=F32)
            ys.append(both[0:CHUNK] + yh_ref[c, q])
            zs[q] = both[CHUNK:] + ql_ref[c, q]
        y_ref[c * CHUNK:(c + 1) * CHUNK, :] = jnp.concatenate(ys, axis=1)

    scan_one_chunk()
    r = r_ref[...]
    k = k_ref[...]
    v = v_ref[...]
    logw = -DECAY_SCALE * _sigmoid(w0_ref[...] + _bdot(jnp.tanh(wl_ref[...]), wup_ref[...]))
    a = _sigmoid(a0_ref[...] + _bdot(al_ref[...], aup_ref[...]))
    scan_one_chunk()
    yield "prep"
    kk = k * kk_ref[...]
    kk = kk / jnp.maximum(jnp.sqrt(_head_sum(kk * kk, hm_ref[...])), 1e-12)
    kd = k * (1.0 + (a - 1.0) * ka_ref[...])
    bb = kk * a
    scan_one_chunk()
    yield "prep"
    cums = [sum(jnp.dot(tri, p, preferred_element_type=F32) for p in _split(logw[rows[c]], 3))
            for c in range(nchunk)]
    scan_one_chunk()
    yield "prep"
    dec = []
    for c in range(nchunk):
        cum, lw = cums[c], logw[rows[c]]
        cmid = cum[CHUNK // 2:CHUNK // 2 + 1]
        ctot = cum[0:1] if reverse else cum[CHUNK - 1:CHUNK]
        e1 = jnp.exp(cum - cmid)
        e1i = jnp.exp(cmid - cum)
        ee = jnp.exp(ctot - cum)
        emid = jnp.exp(cmid)
        r_g = r[rows[c]] * e1
        a_g = -kk[rows[c]] * (e1 * jnp.exp(-lw))
        dec.append(dict(
            r_g=r_g, a_g=a_g, b_g=bb[rows[c]] * e1i, k_g=kd[rows[c]] * e1i, r_s=r_g * emid,
            a_s=a_g * emid, b_e=bb[rows[c]] * ee, k_e=kd[rows[c]] * ee, v=v[rows[c]],
            wtot=jnp.exp(ctot)))
        yield "prep"

    tr = {}
    for name in ("b_g", "k_g", "b_e", "k_e"):
        for c in range(0, nchunk, 2):
            for q in range(ngroup):
                tr[name, c, q], tr[name, c + 1, q] = _head_transpose_pair(
                    dec[c][name][:, lanes[q]], dec[c + 1][name][:, lanes[q]], low)
        yield "prep"
    yield "prep_done"

    st_v = [st(dec[c]["v"][:, lanes[q]]) for c, q in chains]
    grams = [_bdot(jnp.concatenate([dec[c]["a_g"][:, lanes[q]], dec[c]["r_g"][:, lanes[q]]], axis=0),
                   jnp.concatenate([st(tr["b_g", c, q]), st(tr["k_g", c, q])], axis=1))
             for c, q in chains]
    scan_one_chunk()
    yield "stage"
    a_ab = [jnp.where(strict, g_[0:CHUNK, 0:MXU_DIM], 0.0) for g_ in grams]
    a_kr = [jnp.concatenate([jnp.where(strict, g_[0:CHUNK, MXU_DIM:], 0.0),
                             jnp.where(incl, g_[CHUNK:, MXU_DIM:], 0.0)], axis=0) for g_ in grams]
    a_rb = [jnp.where(incl, g_[CHUNK:, 0:MXU_DIM], 0.0) for g_ in grams]
    tinv = [eye_lc + a_ for a_ in a_ab]
    apow = [_bdot(a_, st(a_)) for a_ in a_ab]
    scan_one_chunk()
    yield "stage"
    npow = CHUNK.bit_length() - 2
    for jj in range(npow):
        if jj < npow - 1:
            both = [_bdot(jnp.concatenate([ap, ti_], axis=0), st(ap)) for ap, ti_ in zip(apow, tinv)]
            apow = [b_[0:CHUNK] for b_ in both]
            tinv = [ti_ + b_[CHUNK:] for ti_, b_ in zip(tinv, both)]
        else:
            tinv = [ti_ + _bdot(ti_, st(ap)) for ap, ti_ in zip(apow, tinv)]
        scan_one_chunk()
        yield "stage"
    xyq = [_bdot(jnp.concatenate([a_kr[i], tr["k_e", c, q]], axis=0), st_v[i])
           for i, (c, q) in enumerate(chains)]
    xy = [x[0:2 * CHUNK] for x in xyq]
    q2 = [x[2 * CHUNK:] for x in xyq]
    yield "stage"
    au = [_bdot(tinv[i], jnp.concatenate([st(dec[c]["a_s"][:, lanes[q]]), st(xy[i][0:CHUNK])], axis=1))
          for i, (c, q) in enumerate(chains)]
    while pending:
        scan_one_chunk()
    yield "stage"
    st_au = [jnp.concatenate([st(x[:, 0:MXU_DIM]), st(x[:, MXU_DIM:])], axis=1) for x in au]
    rypq = [_bdot(jnp.concatenate([a_rb[i], tr["b_e", c, q]], axis=0), st_au[i])
            for i, (c, q) in enumerate(chains)]
    ry = [x[0:CHUNK] for x in rypq]
    pq = [x[CHUNK:] for x in rypq]
    for q in range(ngroup):
        z_ref[q] = zs[q]
    for i, (c, q) in enumerate(chains):
        r_hat = dec[c]["r_s"][:, lanes[q]] + ry[i][:, 0:MXU_DIM]
        p_lc = eye_lc * dec[c]["wtot"][:, lanes[q]] + pq[i][:, 0:MXU_DIM]
        rp_ref[c, q] = jnp.concatenate([r_hat, p_lc], axis=0).astype(BF16)
        yh_ref[c, q] = ry[i][:, MXU_DIM:] + xy[i][CHUNK:]
        ql_ref[c, q] = pq[i][:, MXU_DIM:] + q2[i]
    yield "stage"


_WKV_DIR_INPUTS = 9
_WKV_DIR_SCRATCH = 4


def _wkv_pair_kernel(*refs, nsteps):
    ni, ns = _WKV_DIR_INPUTS, _WKV_DIR_SCRATCH
    fwd_in, bwd_in, shared = refs[0:ni], refs[ni:2 * ni], refs[2 * ni:2 * ni + 3]
    y_fwd, y_bwd = refs[2 * ni + 3:2 * ni + 5]
    scratch = refs[2 * ni + 5:]
    g = pl.program_id(0)

    @pl.when(g == 0)
    def _():
        for ref in scratch:
            ref[...] = jnp.zeros_like(ref)

    fwd = _wkv_program(g, *fwd_in, *shared, y_fwd, *scratch[0:ns], reverse=False, nsteps=nsteps)
    bwd = _wkv_program(g, *bwd_in, *shared, y_bwd, *scratch[ns:2 * ns], reverse=True, nsteps=nsteps)
    for tag in fwd:
        if tag == "prep_done":
            break
    bwd_preparing = True
    for _ in fwd:
        if bwd_preparing:
            bwd_preparing = next(bwd) != "prep_done"
    while bwd_preparing:
        bwd_preparing = next(bwd) != "prep_done"
    for _ in bwd:
        pass


_UM_WL_BLOCK = 3 * RW_WIDTH // LANES
_UM_AL_BLOCK = _UM_WL_BLOCK + 1
_UM_GL_BLOCK = _UM_WL_BLOCK + 2


def _wkv_pair(um, d_fwd, d_bwd):
    b, t, _ = um.shape
    ts = min(WKV_STEP, t)
    nsteps = t // ts
    nblocks = b * nsteps
    nchunk = ts // CHUNK
    ngroup = RW_WIDTH // MXU_DIM
    const = lambda g: (0, 0)
    vec = pl.BlockSpec((1, RW_WIDTH), const)

    def dir_specs(reverse):
        pos = (lambda si: nsteps - 1 - si) if reverse else (lambda si: si)
        block_of = lambda gi, col: (gi // nsteps, pos(gi % nsteps), col)
        wide = lambda col: pl.BlockSpec((None, ts, RW_WIDTH),
                                        lambda g: block_of(jnp.minimum(g, nblocks - 1), col))
        narrow = lambda col: pl.BlockSpec((None, ts, LANES),
                                          lambda g: block_of(jnp.minimum(g, nblocks - 1), col))
        ins = [wide(0), wide(1), wide(2), narrow(_UM_WL_BLOCK), narrow(_UM_AL_BLOCK), vec,
               pl.BlockSpec((2 * DECAY_LORA, RW_WIDTH), const), vec,
               pl.BlockSpec((2 * AAA_LORA, RW_WIDTH), const)]
        out = pl.BlockSpec((None, ts, RW_WIDTH), lambda g: block_of(jnp.maximum(g - 1, 0), 0))
        return ins, out

    dir_args = lambda wp: [um, um, um, um, um, wp["w0"], wp["wup"], wp["a0"], wp["aup"]]
    fwd_ins, fwd_out = dir_specs(False)
    bwd_ins, bwd_out = dir_specs(True)
    dir_scratch = [pltpu.VMEM((ngroup, HEAD_DIM, MXU_DIM), F32),
                   pltpu.VMEM((nchunk, ngroup, 2 * CHUNK, MXU_DIM), BF16),
                   pltpu.VMEM((nchunk, ngroup, CHUNK, MXU_DIM), F32),
                   pltpu.VMEM((nchunk, ngroup, HEAD_DIM, MXU_DIM), F32)]
    shp = jax.ShapeDtypeStruct((b, t, RW_WIDTH), F32)
    return pl.pallas_call(
        functools.partial(_wkv_pair_kernel, nsteps=nsteps),
        grid=(nblocks + 1,),
        in_specs=fwd_ins + bwd_ins + [vec, vec, pl.BlockSpec((MXU_DIM, MXU_DIM), const)],
        out_specs=[fwd_out, bwd_out],
        out_shape=[shp, shp],
        scratch_shapes=dir_scratch + dir_scratch,
        compiler_params=_params("arbitrary"),
        name="wkv",
    )(*dir_args(d_fwd), *dir_args(d_bwd), d_fwd["k_k"], d_fwd["k_a"], d_fwd["hm"])


def _outproj_kernel(h_ref, yna_ref, y0_ref, y1_ref, r_ref, k_ref, v_ref, al_ref, gl_ref,
                    a00_ref, aup0_ref, a01_ref, aup1_ref, ka_ref, rk_ref, gup_ref, lw_ref, lb_ref,
                    hm_ref, w_ref, o_ref):
    hm = hm_ref[...]
    half = h_ref.shape[0] // 2
    groups = [slice(0, half), slice(half, 2 * half)]
    wkv = [y0_ref[rows] + y1_ref[rows] for rows in groups]
    mean = [_head_sum(x, hm) * (1.0 / HEAD_DIM) for x in wkv]
    dev = [x - m_ for x, m_ in zip(wkv, mean)]
    var = [_head_sum(d_ * d_, hm) * (1.0 / HEAD_DIM) for d_ in dev]
    yn = [d_ * lax.rsqrt(v_ + LNX_EPS) * lw_ref[...] + lb_ref[...] for d_, v_ in zip(dev, var)]
    a_sum = [_sigmoid(a00_ref[...] + _bdot(al_ref[rows], aup0_ref[...]))
             + _sigmoid(a01_ref[...] + _bdot(al_ref[rows], aup1_ref[...])) for rows in groups]
    kd_sum = [k_ref[rows] * (2.0 + (a_ - 2.0) * ka_ref[...]) for rows, a_ in zip(groups, a_sum)]
    bonus = [_head_sum(r_ref[rows] * kd * rk_ref[...], hm) * v_ref[rows] for rows, kd in zip(groups, kd_sum)]
    gate = [_bdot(_sigmoid(gl_ref[rows]), gup_ref[...]) for rows in groups]
    for rows, yn_, bonus_, gate_ in zip(groups, yn, bonus, gate):
        y_rw = (yn_ + bonus_) * gate_
        mix = jnp.concatenate([yna_ref[rows], y_rw.astype(BF16)], axis=1)
        o_ref[rows] = h_ref[rows] + jnp.dot(mix, w_ref[...], preferred_element_type=F32)


def _outproj(h, yna, y0, y1, um, lw):
    m, d = h.shape
    tm = min(ROW_TILE, m)
    row = lambda i: (i, 0)
    const = lambda i: (0, 0)
    tok = pl.BlockSpec((tm, RW_WIDTH), row)
    vec = pl.BlockSpec((1, RW_WIDTH), const)
    um_wide = lambda col: pl.BlockSpec((tm, RW_WIDTH), lambda i: (i, col))
    um_narrow = lambda col: pl.BlockSpec((tm, LANES), lambda i: (i, col))
    lora = pl.BlockSpec((2 * AAA_LORA, RW_WIDTH), const)
    d0, d1 = lw["dirs"]
    return pl.pallas_call(
        _outproj_kernel,
        grid=(m // tm,),
        in_specs=[pl.BlockSpec((tm, d), row), pl.BlockSpec((tm, NA_WIDTH), row), tok, tok,
                  um_wide(0), um_wide(1), um_wide(2), um_narrow(_UM_AL_BLOCK), um_narrow(_UM_GL_BLOCK),
                  vec, lora, vec, lora, vec, vec, pl.BlockSpec((GATE_LORA, RW_WIDTH), const), vec, vec,
                  pl.BlockSpec((MXU_DIM, MXU_DIM), const),
                  pl.BlockSpec((NA_WIDTH + RW_WIDTH, d), const)],
        out_specs=pl.BlockSpec((tm, d), row),
        out_shape=jax.ShapeDtypeStruct((m, d), F32),
        compiler_params=_params("parallel"),
        name="outproj",
    )(h, yna, y0, y1, um, um, um, um, um, d0["a0"], d0["aup"], d1["a0"], d1["aup"], d0["k_a"],
      lw["r_k"], lw["gup"], lw["lnx_w"], lw["lnx_b"], lw["hm"], lw["w_out"])


def _pad_lora(w_up):
    zero = jnp.zeros_like(w_up[0])
    return (jnp.concatenate([w_up[0], zero], axis=0).astype(BF16),
            jnp.concatenate([zero, w_up[1]], axis=0).astype(BF16))


def _layer_weights(w, i):
    row = lambda x: x.reshape(1, -1)
    wup = _pad_lora(w["rw_w_up"][i])
    aup = _pad_lora(w["rw_a_up"][i])
    lane = jnp.arange(MXU_DIM)
    hm = (lane[:, None] // HEAD_DIM == lane[None, :] // HEAD_DIM).astype(BF16)
    common = dict(k_k=row(w["rw_k_k"][i]), k_a=row(w["rw_k_a"][i]), hm=hm)
    dirs = [dict(common, w0=row(w["rw_w0"][i, d]), wup=wup[d], a0=row(w["rw_a0"][i, d]), aup=aup[d])
            for d in range(2)]
    w_in = w["w_in"][i].astype(BF16)
    return dict(
        ffn1=(row(w["ffn1_norm"][i]), w["ffn1_wg"][i].astype(BF16), w["ffn1_wu"][i].astype(BF16),
              w["ffn1_wd"][i].astype(BF16)),
        ffn2=(row(w["ffn2_norm"][i]), w["ffn2_wg"][i].astype(BF16), w["ffn2_wu"][i].astype(BF16),
              w["ffn2_wd"][i].astype(BF16)),
        mix_norm=row(w["mix_norm"][i]), w_qkv=w_in[:, :3 * NA_WIDTH], w_rw=w_in[:, 3 * NA_WIDTH:],
        mu=row(w["rw_mu"][i]), tz=_na_bias_table(w["na_rpb"][i]), dirs=dirs, hm=hm,
        r_k=row(w["rw_r_k"][i]), gup=w["rw_g_up"][i].astype(BF16),
        lnx_w=row(w["rw_lnx_w"][i]), lnx_b=row(w["rw_lnx_b"][i]), w_out=w["w_out"][i].astype(BF16),
        ple=(row(w["ple_norm"][i]), w["ple_gate"][i].astype(BF16), w["ple_up"][i].astype(BF16)),
    )


def _trunk(x, p, layers, final_norm):
    b, t, d = x.shape
    m = b * t
    h = x.reshape(m, d)
    depth = len(layers)
    for i, lw in enumerate(layers):
        h = _ffn(h, *lw["ffn1"])
        qkv, um = _inproj(h, lw["mix_norm"], lw["w_qkv"], lw["w_rw"], lw["mu"], seq_len=t)
        yna = _na(qkv.reshape(b, t, -1), lw["tz"])
        um3 = um.reshape(b, t, -1)
        y0, y1 = _wkv_pair(um3, lw["dirs"][0], lw["dirs"][1])
        h = _outproj(h, yna.reshape(m, -1), y0.reshape(m, -1), y1.reshape(m, -1), um, lw)
        pn, pg, pu = lw["ple"]
        h = _ffn(h, *lw["ffn2"], ple=(p[i].reshape(m, -1), pn, pg, pu),
                 final_norm=final_norm if i == depth - 1 else None)
    return h.reshape(b, t, d)


def kernel(x_prompt, x_sample, p_prompt, p_sample, ffn1_norm, ffn1_wg, ffn1_wu, ffn1_wd, mix_norm, w_in, na_rpb, rw_mu, rw_w0, rw_w_up, rw_a0, rw_a_up, rw_g_up, rw_k_k, rw_k_a, rw_r_k, rw_lnx_w, rw_lnx_b, w_out, ffn2_norm, ffn2_wg, ffn2_wu, ffn2_wd, ple_norm, ple_gate, ple_up, final_norm):
    w = dict(ffn1_norm=ffn1_norm, ffn1_wg=ffn1_wg, ffn1_wu=ffn1_wu, ffn1_wd=ffn1_wd, mix_norm=mix_norm,
             w_in=w_in, na_rpb=na_rpb, rw_mu=rw_mu, rw_w0=rw_w0, rw_w_up=rw_w_up, rw_a0=rw_a0,
             rw_a_up=rw_a_up, rw_g_up=rw_g_up, rw_k_k=rw_k_k, rw_k_a=rw_k_a, rw_r_k=rw_r_k,
             rw_lnx_w=rw_lnx_w, rw_lnx_b=rw_lnx_b, w_out=w_out, ffn2_norm=ffn2_norm, ffn2_wg=ffn2_wg,
             ffn2_wu=ffn2_wu, ffn2_wd=ffn2_wd, ple_norm=ple_norm, ple_gate=ple_gate, ple_up=ple_up)
    layers = [_layer_weights(w, i) for i in range(ffn1_wg.shape[0])]
    fn = final_norm.reshape(1, -1)
    return (_trunk(x_prompt, p_prompt, layers, fn), _trunk(x_sample, p_sample, layers, fn))
```

```python
import functools

import jax
import jax.numpy as jnp
from jax import lax
from jax.experimental import pallas as pl
from jax.experimental.pallas import tpu as pltpu

F32 = jnp.float32
BF16 = jnp.bfloat16

GRID_W = 64
HEAD_DIM = 64
NA_HEADS = 8
RW_HEADS = 8
NA_WIDTH = NA_HEADS * HEAD_DIM
RW_WIDTH = RW_HEADS * HEAD_DIM
WIN_ROWS = 8
WIN_COLS = 16
DECAY_LORA = 64
AAA_LORA = 64
GATE_LORA = 128
RW_COLS = 3 * RW_WIDTH + 2 * DECAY_LORA + 2 * AAA_LORA + GATE_LORA
NORM_EPS = 1e-6
LNX_EPS = 64e-5
DECAY_SCALE = 0.606531
MASK_VALUE = -1e30

LANES = 128
MXU_DIM = 256
VMEM_LIMIT_BYTES = 52 * 1024 * 1024

ROW_TILE = 512
FFN_SUBTILES = 2
NA_ROWS = 8
NA_UNROLL = 4
CHUNK = 64
WKV_STEP = 512
GROUP_HEADS = MXU_DIM // HEAD_DIM


def _params(*sem):
    return pltpu.CompilerParams(dimension_semantics=sem, vmem_limit_bytes=VMEM_LIMIT_BYTES)


def _bdot(a, b):
    return jnp.dot(a.astype(BF16), b.astype(BF16), preferred_element_type=F32)


def _split(x, n):
    parts = []
    rem = x
    for _ in range(n):
        p = rem.astype(BF16)
        parts.append(p)
        rem = rem - p.astype(F32)
    return parts


def _rms(x, g):
    ms = jnp.mean(x * x, axis=-1, keepdims=True)
    return x * lax.rsqrt(ms + NORM_EPS) * g


def _sigmoid(x):
    return 1.0 / (1.0 + jnp.exp(-x))


def _ffn_kernel(*refs, with_ple, with_final):
    if with_ple:
        x_ref, g_ref, wg_ref, wu_ref, wd_ref, p_ref, pn_ref, pg_ref, pu_ref, fn_ref, o_ref = refs
    else:
        x_ref, g_ref, wg_ref, wu_ref, wd_ref, o_ref = refs
    sub = x_ref.shape[0] // FFN_SUBTILES
    rows = [slice(i * sub, (i + 1) * sub) for i in range(FFN_SUBTILES)]
    xns = [_rms(x_ref[r], g_ref[...]).astype(BF16) for r in rows]
    gates = [jnp.dot(xn, wg_ref[...], preferred_element_type=F32) for xn in xns]
    ups = [jnp.dot(xn, wu_ref[...], preferred_element_type=F32) for xn in xns]
    mids = [((a * _sigmoid(a)) * b).astype(BF16) for a, b in zip(gates, ups)]
    downs = [jnp.dot(m, wd_ref[...], preferred_element_type=F32) for m in mids]
    hs = [x_ref[r] + 0.5 * d for r, d in zip(rows, downs)]
    if with_ple:
        pgate = [_sigmoid(_bdot(_rms(h, pn_ref[...]), pg_ref[...])) for h in hs]
        pup = [_bdot(p_ref[r], pu_ref[...]) for r in rows]
        hs = [h + gt * up for h, gt, up in zip(hs, pgate, pup)]
        if with_final:
            hs = [_rms(h, fn_ref[...]) for h in hs]
    for r, h in zip(rows, hs):
        o_ref[r] = h


def _ffn(x, g, wg, wu, wd, ple=None, final_norm=None):
    m, d = x.shape
    dff = wg.shape[1]
    tm = min(ROW_TILE, m)
    row = lambda i: (i, 0)
    const = lambda i: (0, 0)
    resident = lambda shape: pl.BlockSpec(shape, const, pipeline_mode=pl.Buffered(1))
    in_specs = [pl.BlockSpec((tm, d), row), pl.BlockSpec((1, d), const),
                resident((d, dff)), resident((d, dff)), resident((dff, d))]
    args = [x, g, wg, wu, wd]
    if ple is not None:
        p, pn, pg, pu = ple
        fn = final_norm if final_norm is not None else pn
        in_specs += [pl.BlockSpec((tm, p.shape[1]), row), pl.BlockSpec((1, d), const),
                     resident((d, d)), resident((p.shape[1], d)), pl.BlockSpec((1, d), const)]
        args += [p, pn, pg, pu, fn]
    kern = functools.partial(_ffn_kernel, with_ple=ple is not None, with_final=final_norm is not None)
    return pl.pallas_call(
        kern,
        grid=(m // tm,),
        in_specs=in_specs,
        out_specs=pl.BlockSpec((tm, d), row),
        out_shape=jax.ShapeDtypeStruct((m, d), F32),
        compiler_params=_params("parallel"),
        name="ffn",
    )(*args)


def _inproj_kernel(x_ref, xp_ref, xn_ref, g_ref, wq_ref, wr_ref, mu_ref, qkv_ref, um_ref, *, seq_len):
    i = pl.program_id(0)
    tm = x_ref.shape[0]
    half = tm // 2
    g = g_ref[...]
    lane = lax.broadcasted_iota(jnp.int32, (1, 3 * NA_WIDTH), 1)
    scale = jnp.where(lane < NA_WIDTH, HEAD_DIM ** -0.5, 1.0)
    has_prev = (i * tm) % seq_len != 0
    has_next = ((i + 1) * tm) % seq_len != 0
    xn = [_rms(x_ref[0:half], g).astype(BF16), _rms(x_ref[half:tm], g).astype(BF16)]
    before = [_rms(xp_ref[...], g).astype(BF16), _rms(x_ref[half - 8:half], g).astype(BF16)]
    after = [_rms(x_ref[half:half + 8], g).astype(BF16), _rms(xn_ref[...], g).astype(BF16)]
    keep_before = [has_prev, True]
    keep_after = [True, has_next]
    for s in range(2):
        rows = slice(s * half, (s + 1) * half)
        proj = jnp.dot(xn[s], wq_ref[...], preferred_element_type=F32)
        qkv_ref[rows] = (proj * scale).astype(BF16)
        ext = jnp.concatenate([before[s], xn[s], after[s]], axis=0)
        u = jnp.dot(ext, wr_ref[...], preferred_element_type=F32)
        u = jnp.concatenate([jnp.where(keep_before[s], u[0:8], 0.0), u[8:half + 8],
                             jnp.where(keep_after[s], u[half + 8:], 0.0)], axis=0)
        nbr = pltpu.roll(u, 1, axis=0) + pltpu.roll(u, half + 15, axis=0)
        uc = u[8:half + 8]
        um_ref[rows] = uc + mu_ref[...] * (0.5 * nbr[8:half + 8] - uc)


def _inproj(x, g, w_qkv, w_rw, mu, seq_len):
    m, d = x.shape
    tm = min(ROW_TILE, m, seq_len)
    na3 = 3 * NA_WIDTH
    hb = tm // 8
    nhb = m // 8
    const = lambda i: (0, 0)
    return pl.pallas_call(
        functools.partial(_inproj_kernel, seq_len=seq_len),
        grid=(m // tm,),
        in_specs=[
            pl.BlockSpec((tm, d), lambda i: (i, 0)),
            pl.BlockSpec((8, d), lambda i: (jnp.maximum(i * hb - 1, 0), 0)),
            pl.BlockSpec((8, d), lambda i: (jnp.minimum((i + 1) * hb, nhb - 1), 0)),
            pl.BlockSpec((1, d), const),
            pl.BlockSpec((d, na3), const),
            pl.BlockSpec((d, RW_COLS), const),
            pl.BlockSpec((1, RW_COLS), const),
        ],
        out_specs=[
            pl.BlockSpec((tm, na3), lambda i: (i, 0)),
            pl.BlockSpec((tm, RW_COLS), lambda i: (i, 0)),
        ],
        out_shape=[jax.ShapeDtypeStruct((m, na3), BF16), jax.ShapeDtypeStruct((m, RW_COLS), F32)],
        compiler_params=_params("parallel"),
        name="inproj",
    )(x, x, x, g, w_qkv, w_rw, mu)


def _na_bias_table(rpb):
    j = jnp.arange(GRID_W)
    c = jnp.arange(GRID_W)
    cs = jnp.clip(j - WIN_COLS // 2, 0, GRID_W - WIN_COLS)
    valid = (c[None, :] >= cs[:, None]) & (c[None, :] < cs[:, None] + WIN_COLS)
    cidx = jnp.clip(c[None, :] - j[:, None] + (WIN_COLS - 1), 0, 2 * WIN_COLS - 2)
    full = jnp.where(valid, rpb[:, :, cidx], MASK_VALUE)
    nri = 2 * WIN_ROWS - 2
    two = jnp.concatenate([full[:, 0:nri], full[:, 1:nri + 1]], axis=-1)
    two = two.reshape(NA_HEADS // 2, 2, nri, GRID_W, 2 * GRID_W)
    return jnp.transpose(two, (0, 2, 1, 3, 4)).reshape(NA_HEADS // 2, nri, 2 * GRID_W, 2 * GRID_W)


def _na_window_start(qb, rows):
    return jnp.clip(qb * NA_ROWS - WIN_ROWS // 2, 0, rows - (NA_ROWS + WIN_ROWS))


def _na_kernel(q_ref, k_ref, v_ref, tz_ref, o_ref, *, rows):
    qb = pl.program_id(1)
    first_row = _na_window_start(qb, rows)
    lane = lax.broadcasted_iota(jnp.int32, (GRID_W, 2 * HEAD_DIM), 1)
    low = lane < HEAD_DIM
    win = WIN_ROWS * GRID_W

    npair = NA_HEADS // 2
    sls = [slice(hp * 2 * HEAD_DIM, (hp + 1) * 2 * HEAD_DIM) for hp in range(npair)]
    zero = jnp.zeros((GRID_W, 2 * HEAD_DIM), BF16)

    def body(it, carry):
        qoff, koff, delta = [], [], []
        for n in range(NA_UNROLL):
            il = it * NA_UNROLL + n
            i = qb * NA_ROWS + il
            rs = jnp.clip(i - WIN_ROWS // 2, 0, rows - WIN_ROWS)
            delta.append(i - rs)
            koff.append(pl.multiple_of((rs - first_row) * GRID_W, GRID_W))
            qoff.append(pl.multiple_of(il * GRID_W, GRID_W))
        chains = [(n, hp) for n in range(NA_UNROLL) for hp in range(npair)]
        qps = [q_ref[pl.ds(qoff[n], GRID_W), sls[hp]] for n, hp in chains]
        qsts = [jnp.concatenate([jnp.where(low, qp, zero), jnp.where(low, zero, qp)], axis=0)
                for qp in qps]
        ss = [lax.dot_general(qsts[ci], k_ref[0, pl.ds(koff[n], win), sls[hp]], (((1,), (1,)), ((), ())),
                              preferred_element_type=F32)
              for ci, (n, hp) in enumerate(chains)]
        ss = [ss[ci] + jnp.concatenate(
            [tz_ref[hp, 2 * p - delta[n] + (WIN_ROWS - 1)] for p in range(WIN_ROWS // 2)], axis=1)
            for ci, (n, hp) in enumerate(chains)]
        ps = [jnp.exp(s - jnp.max(s, axis=1, keepdims=True)) for s in ss]
        ps = [p * (1.0 / jnp.sum(p, axis=1, keepdims=True)) for p in ps]
        os_ = [jnp.dot(ps[ci].astype(BF16), v_ref[0, pl.ds(koff[n], win), sls[hp]],
                       preferred_element_type=F32) for ci, (n, hp) in enumerate(chains)]
        outs = [jnp.where(low, o[0:GRID_W], o[GRID_W:2 * GRID_W]) for o in os_]
        for n in range(NA_UNROLL):
            o_ref[pl.ds(qoff[n], GRID_W), :] = jnp.concatenate(
                outs[n * npair:(n + 1) * npair], axis=1).astype(BF16)
        return carry

    lax.fori_loop(0, NA_ROWS // NA_UNROLL, body, 0)


def _na(qkv, tz):
    b, t, _ = qkv.shape
    rows = t // GRID_W
    nqb = rows // NA_ROWS
    blk = NA_ROWS * GRID_W
    kv_spec = lambda col: pl.BlockSpec(
        (pl.Element(1), pl.Element((NA_ROWS + WIN_ROWS) * GRID_W), pl.Element(NA_WIDTH)),
        lambda bi, qi: (bi, _na_window_start(qi, rows) * GRID_W, col * NA_WIDTH))
    return pl.pallas_call(
        functools.partial(_na_kernel, rows=rows),
        grid=(b, nqb),
        in_specs=[pl.BlockSpec((None, blk, NA_WIDTH), lambda bi, qi: (bi, qi, 0)), kv_spec(1), kv_spec(2),
                  pl.BlockSpec(tz.shape, lambda bi, qi: (0, 0, 0, 0))],
        out_specs=pl.BlockSpec((None, blk, NA_WIDTH), lambda bi, qi: (bi, qi, 0)),
        out_shape=jax.ShapeDtypeStruct((b, t, NA_WIDTH), BF16),
        compiler_params=_params("parallel", "arbitrary"),
        name="natten",
    )(qkv, qkv, qkv, tz)


def _head_sum(x, hm):
    return jnp.concatenate([_bdot(x[:, q * MXU_DIM:(q + 1) * MXU_DIM], hm)
                            for q in range(RW_WIDTH // MXU_DIM)], axis=1)


def _stack(x, low):
    xb = x.astype(BF16)
    zero = jnp.zeros((x.shape[0], LANES), BF16)
    blocks = []
    for h in range(GROUP_HEADS):
        tile = xb[:, (h // 2) * LANES:(h // 2 + 1) * LANES]
        kept = jnp.where(low, tile, zero) if h % 2 == 0 else jnp.where(low, zero, tile)
        blocks.append(jnp.concatenate([kept, zero] if h < 2 else [zero, kept], axis=1))
    return jnp.concatenate(blocks, axis=0)


def _head_transpose_pair(x0, x1, low):
    xt = jnp.concatenate([x0, x1], axis=0).T
    b = [xt[h * HEAD_DIM:(h + 1) * HEAD_DIM] for h in range(GROUP_HEADS)]
    rolled = [pltpu.roll(bh, HEAD_DIM, axis=1) for bh in b]
    y0 = jnp.concatenate([jnp.where(low, b[0], rolled[1]), jnp.where(low, b[2], rolled[3])], axis=1)
    y1 = jnp.concatenate([jnp.where(low, rolled[0], b[1]), jnp.where(low, rolled[2], b[3])], axis=1)
    return y0, y1


def _wkv_program(g, r_ref, k_ref, v_ref, wl_ref, al_ref, w0_ref, wup_ref, a0_ref, aup_ref,
                 kk_ref, ka_ref, hm_ref, y_ref, z_ref, rp_ref, yh_ref, ql_ref, *, reverse, nsteps):
    ts = r_ref.shape[0]
    nchunk = ts // CHUNK
    ngroup = RW_WIDTH // MXU_DIM

    ti = lax.broadcasted_iota(jnp.int32, (CHUNK, CHUNK), 0)
    si = lax.broadcasted_iota(jnp.int32, (CHUNK, CHUNK), 1)
    tri = jnp.where((si >= ti) if reverse else (si <= ti), 1.0, 0.0).astype(BF16)
    lane = lax.broadcasted_iota(jnp.int32, (CHUNK, MXU_DIM), 1)
    trow = lax.broadcasted_iota(jnp.int32, (CHUNK, MXU_DIM), 0)
    s_of_lane = lane % HEAD_DIM
    if reverse:
        strict, incl = s_of_lane > trow, s_of_lane >= trow
    else:
        strict, incl = s_of_lane < trow, s_of_lane <= trow
    eye_lc = jnp.where(s_of_lane == trow, 1.0, 0.0)
    low = lax.broadcasted_iota(jnp.int32, (CHUNK, LANES), 1) < HEAD_DIM

    chains = [(c, q) for c in range(nchunk) for q in range(ngroup)]
    lanes = [slice(q * MXU_DIM, (q + 1) * MXU_DIM) for q in range(ngroup)]
    rows = [slice(c * CHUNK, (c + 1) * CHUNK) for c in range(nchunk)]
    st = lambda x: _stack(x, low)

    starts_sequence = (g + nsteps - 1) % nsteps == 0
    zs = [jnp.where(starts_sequence, 0.0, z_ref[q]) for q in range(ngroup)]
    pending = list(range(nchunk - 1, -1, -1) if reverse else range(nchunk))

    def scan_one_chunk():
        if not pending:
            return
        c = pending.pop(0)
        ys = []
        for q in range(ngroup):
            both = jnp.dot(rp_ref[c, q], st(zs[q]), preferred_element_type=F32)
            ys.append(both[0:CHUNK] + yh_ref[c, q])
            zs[q] = both[CHUNK:] + ql_ref[c, q]
        y_ref[c * CHUNK:(c + 1) * CHUNK, :] = jnp.concatenate(ys, axis=1)

    scan_one_chunk()
    r = r_ref[...]
    k = k_ref[...]
    v = v_ref[...]
    logw = -DECAY_SCALE * _sigmoid(w0_ref[...] + _bdot(jnp.tanh(wl_ref[...]), wup_ref[...]))
    a = _sigmoid(a0_ref[...] + _bdot(al_ref[...], aup_ref[...]))
    scan_one_chunk()
    yield "prep"
    kk = k * kk_ref[...]
    kk = kk / jnp.maximum(jnp.sqrt(_head_sum(kk * kk, hm_ref[...])), 1e-12)
    kd = k * (1.0 + (a - 1.0) * ka_ref[...])
    bb = kk * a
    scan_one_chunk()
    yield "prep"
    cums = [sum(jnp.dot(tri, p, preferred_element_type=F32) for p in _split(logw[rows[c]], 3))
            for c in range(nchunk)]
    scan_one_chunk()
    yield "prep"
    dec = []
    for c in range(nchunk):
        cum, lw = cums[c], logw[rows[c]]
        cmid = cum[CHUNK // 2:CHUNK // 2 + 1]
        ctot = cum[0:1] if reverse else cum[CHUNK - 1:CHUNK]
        e1 = jnp.exp(cum - cmid)
        e1i = jnp.exp(cmid - cum)
        ee = jnp.exp(ctot - cum)
        emid = jnp.exp(cmid)
        r_g = r[rows[c]] * e1
        a_g = -kk[rows[c]] * (e1 * jnp.exp(-lw))
        dec.append(dict(
            r_g=r_g, a_g=a_g, b_g=bb[rows[c]] * e1i, k_g=kd[rows[c]] * e1i, r_s=r_g * emid,
            a_s=a_g * emid, b_e=bb[rows[c]] * ee, k_e=kd[rows[c]] * ee, v=v[rows[c]],
            wtot=jnp.exp(ctot)))
        yield "prep"

    tr = {}
    for name in ("b_g", "k_g", "b_e", "k_e"):
        for c in range(0, nchunk, 2):
            for q in range(ngroup):
                tr[name, c, q], tr[name, c + 1, q] = _head_transpose_pair(
                    dec[c][name][:, lanes[q]], dec[c + 1][name][:, lanes[q]], low)
        yield "prep"
    yield "prep_done"

    st_v = [st(dec[c]["v"][:, lanes[q]]) for c, q in chains]
    grams = [_bdot(jnp.concatenate([dec[c]["a_g"][:, lanes[q]], dec[c]["r_g"][:, lanes[q]]], axis=0),
                   jnp.concatenate([st(tr["b_g", c, q]), st(tr["k_g", c, q])], axis=1))
             for c, q in chains]
    scan_one_chunk()
    yield "stage"
    a_ab = [jnp.where(strict, g_[0:CHUNK, 0:MXU_DIM], 0.0) for g_ in grams]
    a_kr = [jnp.concatenate([jnp.where(strict, g_[0:CHUNK, MXU_DIM:], 0.0),
                             jnp.where(incl, g_[CHUNK:, MXU_DIM:], 0.0)], axis=0) for g_ in grams]
    a_rb = [jnp.where(incl, g_[CHUNK:, 0:MXU_DIM], 0.0) for g_ in grams]
    tinv = [eye_lc + a_ for a_ in a_ab]
    apow = [_bdot(a_, st(a_)) for a_ in a_ab]
    scan_one_chunk()
    yield "stage"
    npow = CHUNK.bit_length() - 2
    for jj in range(npow):
        if jj < npow - 1:
            both = [_bdot(jnp.concatenate([ap, ti_], axis=0), st(ap)) for ap, ti_ in zip(apow, tinv)]
            apow = [b_[0:CHUNK] for b_ in both]
            tinv = [ti_ + b_[CHUNK:] for ti_, b_ in zip(tinv, both)]
        else:
            tinv = [ti_ + _bdot(ti_, st(ap)) for ap, ti_ in zip(apow, tinv)]
        scan_one_chunk()
        yield "stage"
    xyq = [_bdot(jnp.concatenate([a_kr[i], tr["k_e", c, q]], axis=0), st_v[i])
           for i, (c, q) in enumerate(chains)]
    xy = [x[0:2 * CHUNK] for x in xyq]
    q2 = [x[2 * CHUNK:] for x in xyq]
    yield "stage"
    au = [_bdot(tinv[i], jnp.concatenate([st(dec[c]["a_s"][:, lanes[q]]), st(xy[i][0:CHUNK])], axis=1))
          for i, (c, q) in enumerate(chains)]
    while pending:
        scan_one_chunk()
    yield "stage"
    st_au = [jnp.concatenate([st(x[:, 0:MXU_DIM]), st(x[:, MXU_DIM:])], axis=1) for x in au]
    rypq = [_bdot(jnp.concatenate([a_rb[i], tr["b_e", c, q]], axis=0), st_au[i])
            for i, (c, q) in enumerate(chains)]
    ry = [x[0:CHUNK] for x in rypq]
    pq = [x[CHUNK:] for x in rypq]
    for q in range(ngroup):
        z_ref[q] = zs[q]
    for i, (c, q) in enumerate(chains):
        r_hat = dec[c]["r_s"][:, lanes[q]] + ry[i][:, 0:MXU_DIM]
        p_lc = eye_lc * dec[c]["wtot"][:, lanes[q]] + pq[i][:, 0:MXU_DIM]
        rp_ref[c, q] = jnp.concatenate([r_hat, p_lc], axis=0).astype(BF16)
        yh_ref[c, q] = ry[i][:, MXU_DIM:] + xy[i][CHUNK:]
        ql_ref[c, q] = pq[i][:, MXU_DIM:] + q2[i]
    yield "stage"


_WKV_DIR_INPUTS = 9
_WKV_DIR_SCRATCH = 4


def _wkv_pair_kernel(*refs, nsteps):
    ni, ns = _WKV_DIR_INPUTS, _WKV_DIR_SCRATCH
    fwd_in, bwd_in, shared = refs[0:ni], refs[ni:2 * ni], refs[2 * ni:2 * ni + 3]
    y_fwd, y_bwd = refs[2 * ni + 3:2 * ni + 5]
    scratch = refs[2 * ni + 5:]
    g = pl.program_id(0)

    @pl.when(g == 0)
    def _():
        for ref in scratch:
            ref[...] = jnp.zeros_like(ref)

    fwd = _wkv_program(g, *fwd_in, *shared, y_fwd, *scratch[0:ns], reverse=False, nsteps=nsteps)
    bwd = _wkv_program(g, *bwd_in, *shared, y_bwd, *scratch[ns:2 * ns], reverse=True, nsteps=nsteps)
    for tag in fwd:
        if tag == "prep_done":
            break
    bwd_preparing = True
    for _ in fwd:
        if bwd_preparing:
            bwd_preparing = next(bwd) != "prep_done"
    while bwd_preparing:
        bwd_preparing = next(bwd) != "prep_done"
    for _ in bwd:
        pass


_UM_WL_BLOCK = 3 * RW_WIDTH // LANES
_UM_AL_BLOCK = _UM_WL_BLOCK + 1
_UM_GL_BLOCK = _UM_WL_BLOCK + 2


def _wkv_pair(um, d_fwd, d_bwd):
    b, t, _ = um.shape
    ts = min(WKV_STEP, t)
    nsteps = t // ts
    nblocks = b * nsteps
    nchunk = ts // CHUNK
    ngroup = RW_WIDTH // MXU_DIM
    const = lambda g: (0, 0)
    vec = pl.BlockSpec((1, RW_WIDTH), const)

    def dir_specs(reverse):
        pos = (lambda si: nsteps - 1 - si) if reverse else (lambda si: si)
        block_of = lambda gi, col: (gi // nsteps, pos(gi % nsteps), col)
        wide = lambda col: pl.BlockSpec((None, ts, RW_WIDTH),
                                        lambda g: block_of(jnp.minimum(g, nblocks - 1), col))
        narrow = lambda col: pl.BlockSpec((None, ts, LANES),
                                          lambda g: block_of(jnp.minimum(g, nblocks - 1), col))
        ins = [wide(0), wide(1), wide(2), narrow(_UM_WL_BLOCK), narrow(_UM_AL_BLOCK), vec,
               pl.BlockSpec((2 * DECAY_LORA, RW_WIDTH), const), vec,
               pl.BlockSpec((2 * AAA_LORA, RW_WIDTH), const)]
        out = pl.BlockSpec((None, ts, RW_WIDTH), lambda g: block_of(jnp.maximum(g - 1, 0), 0))
        return ins, out

    dir_args = lambda wp: [um, um, um, um, um, wp["w0"], wp["wup"], wp["a0"], wp["aup"]]
    fwd_ins, fwd_out = dir_specs(False)
    bwd_ins, bwd_out = dir_specs(True)
    dir_scratch = [pltpu.VMEM((ngroup, HEAD_DIM, MXU_DIM), F32),
                   pltpu.VMEM((nchunk, ngroup, 2 * CHUNK, MXU_DIM), BF16),
                   pltpu.VMEM((nchunk, ngroup, CHUNK, MXU_DIM), F32),
                   pltpu.VMEM((nchunk, ngroup, HEAD_DIM, MXU_DIM), F32)]
    shp = jax.ShapeDtypeStruct((b, t, RW_WIDTH), F32)
    return pl.pallas_call(
        functools.partial(_wkv_pair_kernel, nsteps=nsteps),
        grid=(nblocks + 1,),
        in_specs=fwd_ins + bwd_ins + [vec, vec, pl.BlockSpec((MXU_DIM, MXU_DIM), const)],
        out_specs=[fwd_out, bwd_out],
        out_shape=[shp, shp],
        scratch_shapes=dir_scratch + dir_scratch,
        compiler_params=_params("arbitrary"),
        name="wkv",
    )(*dir_args(d_fwd), *dir_args(d_bwd), d_fwd["k_k"], d_fwd["k_a"], d_fwd["hm"])


def _outproj_kernel(h_ref, yna_ref, y0_ref, y1_ref, r_ref, k_ref, v_ref, al_ref, gl_ref,
                    a00_ref, aup0_ref, a01_ref, aup1_ref, ka_ref, rk_ref, gup_ref, lw_ref, lb_ref,
                    hm_ref, w_ref, o_ref):
    hm = hm_ref[...]
    half = h_ref.shape[0] // 2
    groups = [slice(0, half), slice(half, 2 * half)]
    wkv = [y0_ref[rows] + y1_ref[rows] for rows in groups]
    mean = [_head_sum(x, hm) * (1.0 / HEAD_DIM) for x in wkv]
    dev = [x - m_ for x, m_ in zip(wkv, mean)]
    var = [_head_sum(d_ * d_, hm) * (1.0 / HEAD_DIM) for d_ in dev]
    yn = [d_ * lax.rsqrt(v_ + LNX_EPS) * lw_ref[...] + lb_ref[...] for d_, v_ in zip(dev, var)]
    a_sum = [_sigmoid(a00_ref[...] + _bdot(al_ref[rows], aup0_ref[...]))
             + _sigmoid(a01_ref[...] + _bdot(al_ref[rows], aup1_ref[...])) for rows in groups]
    kd_sum = [k_ref[rows] * (2.0 + (a_ - 2.0) * ka_ref[...]) for rows, a_ in zip(groups, a_sum)]
    bonus = [_head_sum(r_ref[rows] * kd * rk_ref[...], hm) * v_ref[rows] for rows, kd in zip(groups, kd_sum)]
    gate = [_bdot(_sigmoid(gl_ref[rows]), gup_ref[...]) for rows in groups]
    for rows, yn_, bonus_, gate_ in zip(groups, yn, bonus, gate):
        y_rw = (yn_ + bonus_) * gate_
        mix = jnp.concatenate([yna_ref[rows], y_rw.astype(BF16)], axis=1)
        o_ref[rows] = h_ref[rows] + jnp.dot(mix, w_ref[...], preferred_element_type=F32)


def _outproj(h, yna, y0, y1, um, lw):
    m, d = h.shape
    tm = min(ROW_TILE, m)
    row = lambda i: (i, 0)
    const = lambda i: (0, 0)
    tok = pl.BlockSpec((tm, RW_WIDTH), row)
    vec = pl.BlockSpec((1, RW_WIDTH), const)
    um_wide = lambda col: pl.BlockSpec((tm, RW_WIDTH), lambda i: (i, col))
    um_narrow = lambda col: pl.BlockSpec((tm, LANES), lambda i: (i, col))
    lora = pl.BlockSpec((2 * AAA_LORA, RW_WIDTH), const)
    d0, d1 = lw["dirs"]
    return pl.pallas_call(
        _outproj_kernel,
        grid=(m // tm,),
        in_specs=[pl.BlockSpec((tm, d), row), pl.BlockSpec((tm, NA_WIDTH), row), tok, tok,
                  um_wide(0), um_wide(1), um_wide(2), um_narrow(_UM_AL_BLOCK), um_narrow(_UM_GL_BLOCK),
                  vec, lora, vec, lora, vec, vec, pl.BlockSpec((GATE_LORA, RW_WIDTH), const), vec, vec,
                  pl.BlockSpec((MXU_DIM, MXU_DIM), const),
                  pl.BlockSpec((NA_WIDTH + RW_WIDTH, d), const)],
        out_specs=pl.BlockSpec((tm, d), row),
        out_shape=jax.ShapeDtypeStruct((m, d), F32),
        compiler_params=_params("parallel"),
        name="outproj",
    )(h, yna, y0, y1, um, um, um, um, um, d0["a0"], d0["aup"], d1["a0"], d1["aup"], d0["k_a"],
      lw["r_k"], lw["gup"], lw["lnx_w"], lw["lnx_b"], lw["hm"], lw["w_out"])


def _pad_lora(w_up):
    zero = jnp.zeros_like(w_up[0])
    return (jnp.concatenate([w_up[0], zero], axis=0).astype(BF16),
            jnp.concatenate([zero, w_up[1]], axis=0).astype(BF16))


def _layer_weights(w, i):
    row = lambda x: x.reshape(1, -1)
    wup = _pad_lora(w["rw_w_up"][i])
    aup = _pad_lora(w["rw_a_up"][i])
    lane = jnp.arange(MXU_DIM)
    hm = (lane[:, None] // HEAD_DIM == lane[None, :] // HEAD_DIM).astype(BF16)
    common = dict(k_k=row(w["rw_k_k"][i]), k_a=row(w["rw_k_a"][i]), hm=hm)
    dirs = [dict(common, w0=row(w["rw_w0"][i, d]), wup=wup[d], a0=row(w["rw_a0"][i, d]), aup=aup[d])
            for d in range(2)]
    w_in = w["w_in"][i].astype(BF16)
    return dict(
        ffn1=(row(w["ffn1_norm"][i]), w["ffn1_wg"][i].astype(BF16), w["ffn1_wu"][i].astype(BF16),
              w["ffn1_wd"][i].astype(BF16)),
        ffn2=(row(w["ffn2_norm"][i]), w["ffn2_wg"][i].astype(BF16), w["ffn2_wu"][i].astype(BF16),
              w["ffn2_wd"][i].astype(BF16)),
        mix_norm=row(w["mix_norm"][i]), w_qkv=w_in[:, :3 * NA_WIDTH], w_rw=w_in[:, 3 * NA_WIDTH:],
        mu=row(w["rw_mu"][i]), tz=_na_bias_table(w["na_rpb"][i]), dirs=dirs, hm=hm,
        r_k=row(w["rw_r_k"][i]), gup=w["rw_g_up"][i].astype(BF16),
        lnx_w=row(w["rw_lnx_w"][i]), lnx_b=row(w["rw_lnx_b"][i]), w_out=w["w_out"][i].astype(BF16),
        ple=(row(w["ple_norm"][i]), w["ple_gate"][i].astype(BF16), w["ple_up"][i].astype(BF16)),
    )


def _trunk(x, p, layers, final_norm):
    b, t, d = x.shape
    m = b * t
    h = x.reshape(m, d)
    depth = len(layers)
    for i, lw in enumerate(layers):
        h = _ffn(h, *lw["ffn1"])
        qkv, um = _inproj(h, lw["mix_norm"], lw["w_qkv"], lw["w_rw"], lw["mu"], seq_len=t)
        yna = _na(qkv.reshape(b, t, -1), lw["tz"])
        um3 = um.reshape(b, t, -1)
        y0, y1 = _wkv_pair(um3, lw["dirs"][0], lw["dirs"][1])
        h = _outproj(h, yna.reshape(m, -1), y0.reshape(m, -1), y1.reshape(m, -1), um, lw)
        pn, pg, pu = lw["ple"]
        h = _ffn(h, *lw["ffn2"], ple=(p[i].reshape(m, -1), pn, pg, pu),
                 final_norm=final_norm if i == depth - 1 else None)
    return h.reshape(b, t, d)


def kernel(x_prompt, x_sample, p_prompt, p_sample, ffn1_norm, ffn1_wg, ffn1_wu, ffn1_wd, mix_norm, w_in, na_rpb, rw_mu, rw_w0, rw_w_up, rw_a0, rw_a_up, rw_g_up, rw_k_k, rw_k_a, rw_r_k, rw_lnx_w, rw_lnx_b, w_out, ffn2_norm, ffn2_wg, ffn2_wu, ffn2_wd, ple_norm, ple_gate, ple_up, final_norm):
    w = dict(ffn1_norm=ffn1_norm, ffn1_wg=ffn1_wg, ffn1_wu=ffn1_wu, ffn1_wd=ffn1_wd, mix_norm=mix_norm,
             w_in=w_in, na_rpb=na_rpb, rw_mu=rw_mu, rw_w0=rw_w0, rw_w_up=rw_w_up, rw_a0=rw_a0,
             rw_a_up=rw_a_up, rw_g_up=rw_g_up, rw_k_k=rw_k_k, rw_k_a=rw_k_a, rw_r_k=rw_r_k,
             rw_lnx_w=rw_lnx_w, rw_lnx_b=rw_lnx_b, w_out=w_out, ffn2_norm=ffn2_norm, ffn2_wg=ffn2_wg,
             ffn2_wu=ffn2_wu, ffn2_wd=ffn2_wd, ple_norm=ple_norm, ple_gate=ple_gate, ple_up=ple_up)
    layers = [_layer_weights(w, i) for i in range(ffn1_wg.shape[0])]
    fn = final_norm.reshape(1, -1)
    return (_trunk(x_prompt, p_prompt, layers, fn), _trunk(x_sample, p_sample, layers, fn))
```

```python
import functools

import jax
import jax.numpy as jnp
from jax import lax
from jax.experimental import pallas as pl
from jax.experimental.pallas import tpu as pltpu

F32 = jnp.float32
BF16 = jnp.bfloat16

GRID_W = 64
HEAD_DIM = 64
NA_HEADS = 8
RW_HEADS = 8
NA_WIDTH = NA_HEADS * HEAD_DIM
RW_WIDTH = RW_HEADS * HEAD_DIM
WIN_ROWS = 8
WIN_COLS = 16
DECAY_LORA = 64
AAA_LORA = 64
GATE_LORA = 128
RW_COLS = 3 * RW_WIDTH + 2 * DECAY_LORA + 2 * AAA_LORA + GATE_LORA
NORM_EPS = 1e-6
LNX_EPS = 64e-5
DECAY_SCALE = 0.606531
MASK_VALUE = -1e30

LANES = 128
MXU_DIM = 256
VMEM_LIMIT_BYTES = 52 * 1024 * 1024

ROW_TILE = 512
FFN_SUBTILES = 2
NA_ROWS = 8
NA_UNROLL = 4
CHUNK = 64
WKV_STEP = 512
GROUP_HEADS = MXU_DIM // HEAD_DIM


def _params(*sem):
    return pltpu.CompilerParams(dimension_semantics=sem, vmem_limit_bytes=VMEM_LIMIT_BYTES)


def _bdot(a, b):
    return jnp.dot(a.astype(BF16), b.astype(BF16), preferred_element_type=F32)


def _split(x, n):
    parts = []
    rem = x
    for _ in range(n):
        p = rem.astype(BF16)
        parts.append(p)
        rem = rem - p.astype(F32)
    return parts


def _rms(x, g):
    ms = jnp.mean(x * x, axis=-1, keepdims=True)
    return x * lax.rsqrt(ms + NORM_EPS) * g


def _sigmoid(x):
    return 1.0 / (1.0 + jnp.exp(-x))


def _ffn_kernel(*refs, with_ple, with_final):
    if with_ple:
        x_ref, g_ref, wg_ref, wu_ref, wd_ref, p_ref, pn_ref, pg_ref, pu_ref, fn_ref, o_ref = refs
    else:
        x_ref, g_ref, wg_ref, wu_ref, wd_ref, o_ref = refs
    sub = x_ref.shape[0] // FFN_SUBTILES
    rows = [slice(i * sub, (i + 1) * sub) for i in range(FFN_SUBTILES)]
    xns = [_rms(x_ref[r], g_ref[...]).astype(BF16) for r in rows]
    gates = [jnp.dot(xn, wg_ref[...], preferred_element_type=F32) for xn in xns]
    ups = [jnp.dot(xn, wu_ref[...], preferred_element_type=F32) for xn in xns]
    mids = [((a * _sigmoid(a)) * b).astype(BF16) for a, b in zip(gates, ups)]
    downs = [jnp.dot(m, wd_ref[...], preferred_element_type=F32) for m in mids]
    hs = [x_ref[r] + 0.5 * d for r, d in zip(rows, downs)]
    if with_ple:
        pgate = [_sigmoid(_bdot(_rms(h, pn_ref[...]), pg_ref[...])) for h in hs]
        pup = [_bdot(p_ref[r], pu_ref[...]) for r in rows]
        hs = [h + gt * up for h, gt, up in zip(hs, pgate, pup)]
        if with_final:
            hs = [_rms(h, fn_ref[...]) for h in hs]
    for r, h in zip(rows, hs):
        o_ref[r] = h


def _ffn(x, g, wg, wu, wd, ple=None, final_norm=None):
    m, d = x.shape
    dff = wg.shape[1]
    tm = min(ROW_TILE, m)
    row = lambda i: (i, 0)
    const = lambda i: (0, 0)
    resident = lambda shape: pl.BlockSpec(shape, const, pipeline_mode=pl.Buffered(1))
    in_specs = [pl.BlockSpec((tm, d), row), pl.BlockSpec((1, d), const),
                resident((d, dff)), resident((d, dff)), resident((dff, d))]
    args = [x, g, wg, wu, wd]
    if ple is not None:
        p, layer, pn, pg, pu = ple
        fn = final_norm if final_norm is not None else pn
        in_specs += [pl.BlockSpec((None, tm, p.shape[2]), lambda i: (layer, i, 0)),
                     pl.BlockSpec((1, d), const),
                     resident((d, d)), resident((p.shape[2], d)), pl.BlockSpec((1, d), const)]
        args += [p, pn, pg, pu, fn]
    kern = functools.partial(_ffn_kernel, with_ple=ple is not None, with_final=final_norm is not None)
    return pl.pallas_call(
        kern,
        grid=(m // tm,),
        in_specs=in_specs,
        out_specs=pl.BlockSpec((tm, d), row),
        out_shape=jax.ShapeDtypeStruct((m, d), F32),
        compiler_params=_params("parallel"),
        name="ffn",
    )(*args)


def _inproj_kernel(x_ref, xp_ref, xn_ref, g_ref, wq_ref, wr_ref, mu_ref, qkv_ref, um_ref, *, seq_len):
    i = pl.program_id(0)
    tm = x_ref.shape[0]
    half = tm // 2
    g = g_ref[...]
    lane = lax.broadcasted_iota(jnp.int32, (1, 3 * NA_WIDTH), 1)
    scale = jnp.where(lane < NA_WIDTH, HEAD_DIM ** -0.5, 1.0)
    has_prev = (i * tm) % seq_len != 0
    has_next = ((i + 1) * tm) % seq_len != 0
    xn = [_rms(x_ref[0:half], g).astype(BF16), _rms(x_ref[half:tm], g).astype(BF16)]
    before = [_rms(xp_ref[...], g).astype(BF16), _rms(x_ref[half - 8:half], g).astype(BF16)]
    after = [_rms(x_ref[half:half + 8], g).astype(BF16), _rms(xn_ref[...], g).astype(BF16)]
    keep_before = [has_prev, True]
    keep_after = [True, has_next]
    for s in range(2):
        rows = slice(s * half, (s + 1) * half)
        proj = jnp.dot(xn[s], wq_ref[...], preferred_element_type=F32)
        qkv_ref[rows] = (proj * scale).astype(BF16)
        ext = jnp.concatenate([before[s], xn[s], after[s]], axis=0)
        u = jnp.dot(ext, wr_ref[...], preferred_element_type=F32)
        u = jnp.concatenate([jnp.where(keep_before[s], u[0:8], 0.0), u[8:half + 8],
                             jnp.where(keep_after[s], u[half + 8:], 0.0)], axis=0)
        nbr = pltpu.roll(u, 1, axis=0) + pltpu.roll(u, half + 15, axis=0)
        uc = u[8:half + 8]
        um_ref[rows] = uc + mu_ref[...] * (0.5 * nbr[8:half + 8] - uc)


def _inproj(x, g, w_qkv, w_rw, mu, seq_len):
    m, d = x.shape
    tm = min(ROW_TILE, m, seq_len)
    na3 = 3 * NA_WIDTH
    hb = tm // 8
    nhb = m // 8
    const = lambda i: (0, 0)
    return pl.pallas_call(
        functools.partial(_inproj_kernel, seq_len=seq_len),
        grid=(m // tm,),
        in_specs=[
            pl.BlockSpec((tm, d), lambda i: (i, 0)),
            pl.BlockSpec((8, d), lambda i: (jnp.maximum(i * hb - 1, 0), 0)),
            pl.BlockSpec((8, d), lambda i: (jnp.minimum((i + 1) * hb, nhb - 1), 0)),
            pl.BlockSpec((1, d), const),
            pl.BlockSpec((d, na3), const),
            pl.BlockSpec((d, RW_COLS), const),
            pl.BlockSpec((1, RW_COLS), const),
        ],
        out_specs=[
            pl.BlockSpec((tm, na3), lambda i: (i, 0)),
            pl.BlockSpec((tm, RW_COLS), lambda i: (i, 0)),
        ],
        out_shape=[jax.ShapeDtypeStruct((m, na3), BF16), jax.ShapeDtypeStruct((m, RW_COLS), F32)],
        compiler_params=_params("parallel"),
        name="inproj",
    )(x, x, x, g, w_qkv, w_rw, mu)


def _na_bias_table(rpb):
    j = jnp.arange(GRID_W)
    c = jnp.arange(GRID_W)
    cs = jnp.clip(j - WIN_COLS // 2, 0, GRID_W - WIN_COLS)
    valid = (c[None, :] >= cs[:, None]) & (c[None, :] < cs[:, None] + WIN_COLS)
    cidx = jnp.clip(c[None, :] - j[:, None] + (WIN_COLS - 1), 0, 2 * WIN_COLS - 2)
    full = jnp.where(valid, rpb[:, :, cidx], MASK_VALUE)
    nri = 2 * WIN_ROWS - 2
    two = jnp.concatenate([full[:, 0:nri], full[:, 1:nri + 1]], axis=-1)
    two = two.reshape(NA_HEADS // 2, 2, nri, GRID_W, 2 * GRID_W)
    return jnp.transpose(two, (0, 2, 1, 3, 4)).reshape(NA_HEADS // 2, nri, 2 * GRID_W, 2 * GRID_W)


def _na_window_start(qb, rows):
    return jnp.clip(qb * NA_ROWS - WIN_ROWS // 2, 0, rows - (NA_ROWS + WIN_ROWS))


def _na_kernel(q_ref, k_ref, v_ref, tz_ref, o_ref, *, rows):
    qb = pl.program_id(1)
    first_row = _na_window_start(qb, rows)
    lane = lax.broadcasted_iota(jnp.int32, (GRID_W, 2 * HEAD_DIM), 1)
    low = lane < HEAD_DIM
    win = WIN_ROWS * GRID_W

    npair = NA_HEADS // 2
    sls = [slice(hp * 2 * HEAD_DIM, (hp + 1) * 2 * HEAD_DIM) for hp in range(npair)]
    zero = jnp.zeros((GRID_W, 2 * HEAD_DIM), BF16)

    def body(it, carry):
        qoff, koff, delta = [], [], []
        for n in range(NA_UNROLL):
            il = it * NA_UNROLL + n
            i = qb * NA_ROWS + il
            rs = jnp.clip(i - WIN_ROWS // 2, 0, rows - WIN_ROWS)
            delta.append(i - rs)
            koff.append(pl.multiple_of((rs - first_row) * GRID_W, GRID_W))
            qoff.append(pl.multiple_of(il * GRID_W, GRID_W))
        chains = [(n, hp) for n in range(NA_UNROLL) for hp in range(npair)]
        qps = [q_ref[pl.ds(qoff[n], GRID_W), sls[hp]] for n, hp in chains]
        qsts = [jnp.concatenate([jnp.where(low, qp, zero), jnp.where(low, zero, qp)], axis=0)
                for qp in qps]
        ss = [lax.dot_general(qsts[ci], k_ref[0, pl.ds(koff[n], win), sls[hp]], (((1,), (1,)), ((), ())),
                              preferred_element_type=F32)
              for ci, (n, hp) in enumerate(chains)]
        ss = [ss[ci] + jnp.concatenate(
            [tz_ref[hp, 2 * p - delta[n] + (WIN_ROWS - 1)] for p in range(WIN_ROWS // 2)], axis=1)
            for ci, (n, hp) in enumerate(chains)]
        ps = [jnp.exp(s - jnp.max(s, axis=1, keepdims=True)) for s in ss]
        ps = [p * (1.0 / jnp.sum(p, axis=1, keepdims=True)) for p in ps]
        os_ = [jnp.dot(ps[ci].astype(BF16), v_ref[0, pl.ds(koff[n], win), sls[hp]],
                       preferred_element_type=F32) for ci, (n, hp) in enumerate(chains)]
        outs = [jnp.where(low, o[0:GRID_W], o[GRID_W:2 * GRID_W]) for o in os_]
        for n in range(NA_UNROLL):
            o_ref[pl.ds(qoff[n], GRID_W), :] = jnp.concatenate(
                outs[n * npair:(n + 1) * npair], axis=1).astype(BF16)
        return carry

    lax.fori_loop(0, NA_ROWS // NA_UNROLL, body, 0)


def _na(qkv, tz):
    b, t, _ = qkv.shape
    rows = t // GRID_W
    nqb = rows // NA_ROWS
    blk = NA_ROWS * GRID_W
    kv_spec = lambda col: pl.BlockSpec(
        (pl.Element(1), pl.Element((NA_ROWS + WIN_ROWS) * GRID_W), pl.Element(NA_WIDTH)),
        lambda bi, qi: (bi, _na_window_start(qi, rows) * GRID_W, col * NA_WIDTH))
    return pl.pallas_call(
        functools.partial(_na_kernel, rows=rows),
        grid=(b, nqb),
        in_specs=[pl.BlockSpec((None, blk, NA_WIDTH), lambda bi, qi: (bi, qi, 0)), kv_spec(1), kv_spec(2),
                  pl.BlockSpec(tz.shape, lambda bi, qi: (0, 0, 0, 0))],
        out_specs=pl.BlockSpec((None, blk, NA_WIDTH), lambda bi, qi: (bi, qi, 0)),
        out_shape=jax.ShapeDtypeStruct((b, t, NA_WIDTH), BF16),
        compiler_params=_params("parallel", "arbitrary"),
        name="natten",
    )(qkv, qkv, qkv, tz)


def _head_sum(x, hm):
    return jnp.concatenate([_bdot(x[:, q * MXU_DIM:(q + 1) * MXU_DIM], hm)
                            for q in range(RW_WIDTH // MXU_DIM)], axis=1)


def _stack(x, low):
    xb = x.astype(BF16)
    zero = jnp.zeros((x.shape[0], LANES), BF16)
    blocks = []
    for h in range(GROUP_HEADS):
        tile = xb[:, (h // 2) * LANES:(h // 2 + 1) * LANES]
        kept = jnp.where(low, tile, zero) if h % 2 == 0 else jnp.where(low, zero, tile)
        blocks.append(jnp.concatenate([kept, zero] if h < 2 else [zero, kept], axis=1))
    return jnp.concatenate(blocks, axis=0)


def _head_transpose_pair(x0, x1, low):
    xt = jnp.concatenate([x0, x1], axis=0).T
    b = [xt[h * HEAD_DIM:(h + 1) * HEAD_DIM] for h in range(GROUP_HEADS)]
    rolled = [pltpu.roll(bh, HEAD_DIM, axis=1) for bh in b]
    y0 = jnp.concatenate([jnp.where(low, b[0], rolled[1]), jnp.where(low, b[2], rolled[3])], axis=1)
    y1 = jnp.concatenate([jnp.where(low, rolled[0], b[1]), jnp.where(low, rolled[2], b[3])], axis=1)
    return y0, y1


def _wkv_scan_program(g, y_ref, z_ref, rp_ref, yh_ref, ql_ref, low, *, reverse, nsteps):
    nchunk, ngroup = rp_ref.shape[0], rp_ref.shape[1]
    starts_sequence = (g + nsteps - 1) % nsteps == 0
    zs = [jnp.where(starts_sequence, 0.0, z_ref[q]) for q in range(ngroup)]
    for c in (range(nchunk - 1, -1, -1) if reverse else range(nchunk)):
        ys = []
        for q in range(ngroup):
            both = jnp.dot(rp_ref[c, q], _stack(zs[q], low), preferred_element_type=F32)
            ys.append(both[0:CHUNK] + yh_ref[c, q])
            zs[q] = both[CHUNK:] + ql_ref[c, q]
        y_ref[c * CHUNK:(c + 1) * CHUNK, :] = jnp.concatenate(ys, axis=1)
        if c == (0 if reverse else nchunk - 1):
            for q in range(ngroup):
                z_ref[q] = zs[q]
        yield "scan"


def _wkv_chunk_program(chunk_ids, r_ref, k_ref, v_ref, wl_ref, al_ref, w0_ref, wup_ref, a0_ref, aup_ref,
                       kk_ref, ka_ref, hm_ref, rp_ref, yh_ref, ql_ref, low, *, reverse):
    ngroup = RW_WIDTH // MXU_DIM
    nloc = len(chunk_ids)
    tok = slice(chunk_ids[0] * CHUNK, (chunk_ids[-1] + 1) * CHUNK)

    ti = lax.broadcasted_iota(jnp.int32, (CHUNK, CHUNK), 0)
    si = lax.broadcasted_iota(jnp.int32, (CHUNK, CHUNK), 1)
    tri = jnp.where((si >= ti) if reverse else (si <= ti), 1.0, 0.0).astype(BF16)
    lane = lax.broadcasted_iota(jnp.int32, (CHUNK, MXU_DIM), 1)
    trow = lax.broadcasted_iota(jnp.int32, (CHUNK, MXU_DIM), 0)
    s_of_lane = lane % HEAD_DIM
    if reverse:
        strict, incl = s_of_lane > trow, s_of_lane >= trow
    else:
        strict, incl = s_of_lane < trow, s_of_lane <= trow
    eye_lc = jnp.where(s_of_lane == trow, 1.0, 0.0)

    chains = [(c, q) for c in range(nloc) for q in range(ngroup)]
    lanes = [slice(q * MXU_DIM, (q + 1) * MXU_DIM) for q in range(ngroup)]
    rows = [slice(c * CHUNK, (c + 1) * CHUNK) for c in range(nloc)]
    st = lambda x: _stack(x, low)

    r = r_ref[tok]
    k = k_ref[tok]
    v = v_ref[tok]
    logw = -DECAY_SCALE * _sigmoid(w0_ref[...] + _bdot(jnp.tanh(wl_ref[tok]), wup_ref[...]))
    a = _sigmoid(a0_ref[...] + _bdot(al_ref[tok], aup_ref[...]))
    yield "prep"
    kk = k * kk_ref[...]
    kk = kk / jnp.maximum(jnp.sqrt(_head_sum(kk * kk, hm_ref[...])), 1e-12)
    kd = k * (1.0 + (a - 1.0) * ka_ref[...])
    bb = kk * a
    yield "prep"
    cums = [sum(jnp.dot(tri, p, preferred_element_type=F32) for p in _split(logw[rows[c]], 3))
            for c in range(nloc)]
    yield "prep"
    dec = []
    for c in range(nloc):
        cum, lw = cums[c], logw[rows[c]]
        cmid = cum[CHUNK // 2:CHUNK // 2 + 1]
        ctot = cum[0:1] if reverse else cum[CHUNK - 1:CHUNK]
        e1 = jnp.exp(cum - cmid)
        e1i = jnp.exp(cmid - cum)
        ee = jnp.exp(ctot - cum)
        emid = jnp.exp(cmid)
        r_g = r[rows[c]] * e1
        a_g = -kk[rows[c]] * (e1 * jnp.exp(-lw))
        dec.append(dict(
            r_g=r_g, a_g=a_g, b_g=bb[rows[c]] * e1i, k_g=kd[rows[c]] * e1i, r_s=r_g * emid,
            a_s=a_g * emid, b_e=bb[rows[c]] * ee, k_e=kd[rows[c]] * ee, v=v[rows[c]],
            wtot=jnp.exp(ctot)))
        yield "prep"

    tr = {}
    for name in ("b_g", "k_g", "b_e", "k_e"):
        for c in range(0, nloc, 2):
            for q in range(ngroup):
                tr[name, c, q], tr[name, c + 1, q] = _head_transpose_pair(
                    dec[c][name][:, lanes[q]], dec[c + 1][name][:, lanes[q]], low)
        yield "prep"
    yield "prep_done"

    st_v = [st(dec[c]["v"][:, lanes[q]]) for c, q in chains]
    grams = [_bdot(jnp.concatenate([dec[c]["a_g"][:, lanes[q]], dec[c]["r_g"][:, lanes[q]]], axis=0),
                   jnp.concatenate([st(tr["b_g", c, q]), st(tr["k_g", c, q])], axis=1))
             for c, q in chains]
    yield "stage"
    a_ab = [jnp.where(strict, g_[0:CHUNK, 0:MXU_DIM], 0.0) for g_ in grams]
    a_kr = [jnp.concatenate([jnp.where(strict, g_[0:CHUNK, MXU_DIM:], 0.0),
                             jnp.where(incl, g_[CHUNK:, MXU_DIM:], 0.0)], axis=0) for g_ in grams]
    a_rb = [jnp.where(incl, g_[CHUNK:, 0:MXU_DIM], 0.0) for g_ in grams]
    tinv = [eye_lc + a_ for a_ in a_ab]
    apow = [_bdot(a_, st(a_)) for a_ in a_ab]
    yield "stage"
    npow = CHUNK.bit_length() - 2
    for jj in range(npow):
        if jj < npow - 1:
            both = [_bdot(jnp.concatenate([ap, ti_], axis=0), st(ap)) for ap, ti_ in zip(apow, tinv)]
            apow = [b_[0:CHUNK] for b_ in both]
            tinv = [ti_ + b_[CHUNK:] for ti_, b_ in zip(tinv, both)]
        else:
            tinv = [ti_ + _bdot(ti_, st(ap)) for ap, ti_ in zip(apow, tinv)]
        yield "stage"
    xyq = [_bdot(jnp.concatenate([a_kr[i], tr["k_e", c, q]], axis=0), st_v[i])
           for i, (c, q) in enumerate(chains)]
    xy = [x[0:2 * CHUNK] for x in xyq]
    q2 = [x[2 * CHUNK:] for x in xyq]
    yield "stage"
    au = [_bdot(tinv[i], jnp.concatenate([st(dec[c]["a_s"][:, lanes[q]]), st(xy[i][0:CHUNK])], axis=1))
          for i, (c, q) in enumerate(chains)]
    yield "stage"
    st_au = [jnp.concatenate([st(x[:, 0:MXU_DIM]), st(x[:, MXU_DIM:])], axis=1) for x in au]
    rypq = [_bdot(jnp.concatenate([a_rb[i], tr["b_e", c, q]], axis=0), st_au[i])
            for i, (c, q) in enumerate(chains)]
    yield "last_stage"
    ry = [x[0:CHUNK] for x in rypq]
    pq = [x[CHUNK:] for x in rypq]
    for i, (c, q) in enumerate(chains):
        r_hat = dec[c]["r_s"][:, lanes[q]] + ry[i][:, 0:MXU_DIM]
        p_lc = eye_lc * dec[c]["wtot"][:, lanes[q]] + pq[i][:, 0:MXU_DIM]
        rp_ref[chunk_ids[c], q] = jnp.concatenate([r_hat, p_lc], axis=0).astype(BF16)
        yh_ref[chunk_ids[c], q] = ry[i][:, MXU_DIM:] + xy[i][CHUNK:]
        ql_ref[chunk_ids[c], q] = pq[i][:, MXU_DIM:] + q2[i]
    yield "stage"


_WKV_DIR_INPUTS = 9
_WKV_DIR_SCRATCH = 4
_WKV_WAVES = 2


def _wkv_pair_kernel(*refs, nsteps):
    ni, ns = _WKV_DIR_INPUTS, _WKV_DIR_SCRATCH
    fwd_in, bwd_in, shared = refs[0:ni], refs[ni:2 * ni], refs[2 * ni:2 * ni + 3]
    y_fwd, y_bwd = refs[2 * ni + 3:2 * ni + 5]
    scratch = refs[2 * ni + 5:]
    g = pl.program_id(0)

    @pl.when(g == 0)
    def _():
        for ref in scratch:
            ref[...] = jnp.zeros_like(ref)

    low = lax.broadcasted_iota(jnp.int32, (CHUNK, LANES), 1) < HEAD_DIM
    nchunk = fwd_in[0].shape[0] // CHUNK
    per_wave = nchunk // _WKV_WAVES
    waves = [list(range(w * per_wave, (w + 1) * per_wave)) for w in range(_WKV_WAVES)]
    z_f, terms_f = scratch[0], scratch[1:ns]
    z_b, terms_b = scratch[ns], scratch[ns + 1:2 * ns]
    scans = [_wkv_scan_program(g, y_fwd, z_f, *terms_f, low, reverse=False, nsteps=nsteps),
             _wkv_scan_program(g, y_bwd, z_b, *terms_b, low, reverse=True, nsteps=nsteps)]
    programs = ([_wkv_chunk_program(w, *fwd_in, *shared, *terms_f, low, reverse=False) for w in waves]
                + [_wkv_chunk_program(w, *bwd_in, *shared, *terms_b, low, reverse=True) for w in waves])

    def scan_step():
        for s in scans:
            next(s, None)

    def drain_scans():
        for s in scans:
            for _ in s:
                pass

    current = programs[0]
    for n, tag in enumerate(current):
        if n % 2 == 1:
            scan_step()
        if tag == "prep_done":
            break
    for nxt in programs[1:] + [None]:
        preparing = nxt is not None
        for tag in current:
            if tag == "last_stage":
                drain_scans()
            else:
                scan_step()
            if preparing:
                preparing = next(nxt) != "prep_done"
        while preparing:
            preparing = next(nxt) != "prep_done"
        current = nxt


_UM_WL_BLOCK = 3 * RW_WIDTH // LANES
_UM_AL_BLOCK = _UM_WL_BLOCK + 1
_UM_GL_BLOCK = _UM_WL_BLOCK + 2


def _wkv_pair(um, d_fwd, d_bwd):
    b, t, _ = um.shape
    ts = min(WKV_STEP, t)
    nsteps = t // ts
    nblocks = b * nsteps
    nchunk = ts // CHUNK
    ngroup = RW_WIDTH // MXU_DIM
    const = lambda g: (0, 0)
    vec = pl.BlockSpec((1, RW_WIDTH), const)

    def dir_specs(reverse):
        pos = (lambda si: nsteps - 1 - si) if reverse else (lambda si: si)
        block_of = lambda gi, col: (gi // nsteps, pos(gi % nsteps), col)
        wide = lambda col: pl.BlockSpec((None, ts, RW_WIDTH),
                                        lambda g: block_of(jnp.minimum(g, nblocks - 1), col))
        narrow = lambda col: pl.BlockSpec((None, ts, LANES),
                                          lambda g: block_of(jnp.minimum(g, nblocks - 1), col))
        ins = [wide(0), wide(1), wide(2), narrow(_UM_WL_BLOCK), narrow(_UM_AL_BLOCK), vec,
               pl.BlockSpec((2 * DECAY_LORA, RW_WIDTH), const), vec,
               pl.BlockSpec((2 * AAA_LORA, RW_WIDTH), const)]
        out = pl.BlockSpec((None, ts, RW_WIDTH), lambda g: block_of(jnp.maximum(g - 1, 0), 0))
        return ins, out

    dir_args = lambda wp: [um, um, um, um, um, wp["w0"], wp["wup"], wp["a0"], wp["aup"]]
    fwd_ins, fwd_out = dir_specs(False)
    bwd_ins, bwd_out = dir_specs(True)
    dir_scratch = [pltpu.VMEM((ngroup, HEAD_DIM, MXU_DIM), F32),
                   pltpu.VMEM((nchunk, ngroup, 2 * CHUNK, MXU_DIM), BF16),
                   pltpu.VMEM((nchunk, ngroup, CHUNK, MXU_DIM), F32),
                   pltpu.VMEM((nchunk, ngroup, HEAD_DIM, MXU_DIM), F32)]
    shp = jax.ShapeDtypeStruct((b, t, RW_WIDTH), F32)
    return pl.pallas_call(
        functools.partial(_wkv_pair_kernel, nsteps=nsteps),
        grid=(nblocks + 1,),
        in_specs=fwd_ins + bwd_ins + [vec, vec, pl.BlockSpec((MXU_DIM, MXU_DIM), const)],
        out_specs=[fwd_out, bwd_out],
        out_shape=[shp, shp],
        scratch_shapes=dir_scratch + dir_scratch,
        compiler_params=_params("arbitrary"),
        name="wkv",
    )(*dir_args(d_fwd), *dir_args(d_bwd), d_fwd["k_k"], d_fwd["k_a"], d_fwd["hm"])


def _outproj_kernel(h_ref, yna_ref, y0_ref, y1_ref, r_ref, k_ref, v_ref, al_ref, gl_ref,
                    a00_ref, aup0_ref, a01_ref, aup1_ref, ka_ref, rk_ref, gup_ref, lw_ref, lb_ref,
                    hm_ref, w_ref, o_ref):
    hm = hm_ref[...]
    half = h_ref.shape[0] // 2
    groups = [slice(0, half), slice(half, 2 * half)]
    wkv = [y0_ref[rows] + y1_ref[rows] for rows in groups]
    mean = [_head_sum(x, hm) * (1.0 / HEAD_DIM) for x in wkv]
    dev = [x - m_ for x, m_ in zip(wkv, mean)]
    var = [_head_sum(d_ * d_, hm) * (1.0 / HEAD_DIM) for d_ in dev]
    yn = [d_ * lax.rsqrt(v_ + LNX_EPS) * lw_ref[...] + lb_ref[...] for d_, v_ in zip(dev, var)]
    a_sum = [_sigmoid(a00_ref[...] + _bdot(al_ref[rows], aup0_ref[...]))
             + _sigmoid(a01_ref[...] + _bdot(al_ref[rows], aup1_ref[...])) for rows in groups]
    kd_sum = [k_ref[rows] * (2.0 + (a_ - 2.0) * ka_ref[...]) for rows, a_ in zip(groups, a_sum)]
    bonus = [_head_sum(r_ref[rows] * kd * rk_ref[...], hm) * v_ref[rows] for rows, kd in zip(groups, kd_sum)]
    gate = [_bdot(_sigmoid(gl_ref[rows]), gup_ref[...]) for rows in groups]
    for rows, yn_, bonus_, gate_ in zip(groups, yn, bonus, gate):
        y_rw = (yn_ + bonus_) * gate_
        mix = jnp.concatenate([yna_ref[rows], y_rw.astype(BF16)], axis=1)
        o_ref[rows] = h_ref[rows] + jnp.dot(mix, w_ref[...], preferred_element_type=F32)


def _outproj(h, yna, y0, y1, um, lw):
    m, d = h.shape
    tm = min(ROW_TILE, m)
    row = lambda i: (i, 0)
    const = lambda i: (0, 0)
    tok = pl.BlockSpec((tm, RW_WIDTH), row)
    vec = pl.BlockSpec((1, RW_WIDTH), const)
    um_wide = lambda col: pl.BlockSpec((tm, RW_WIDTH), lambda i: (i, col))
    um_narrow = lambda col: pl.BlockSpec((tm, LANES), lambda i: (i, col))
    lora = pl.BlockSpec((2 * AAA_LORA, RW_WIDTH), const)
    d0, d1 = lw["dirs"]
    return pl.pallas_call(
        _outproj_kernel,
        grid=(m // tm,),
        in_specs=[pl.BlockSpec((tm, d), row), pl.BlockSpec((tm, NA_WIDTH), row), tok, tok,
                  um_wide(0), um_wide(1), um_wide(2), um_narrow(_UM_AL_BLOCK), um_narrow(_UM_GL_BLOCK),
                  vec, lora, vec, lora, vec, vec, pl.BlockSpec((GATE_LORA, RW_WIDTH), const), vec, vec,
                  pl.BlockSpec((MXU_DIM, MXU_DIM), const),
                  pl.BlockSpec((NA_WIDTH + RW_WIDTH, d), const)],
        out_specs=pl.BlockSpec((tm, d), row),
        out_shape=jax.ShapeDtypeStruct((m, d), F32),
        compiler_params=_params("parallel"),
        name="outproj",
    )(h, yna, y0, y1, um, um, um, um, um, d0["a0"], d0["aup"], d1["a0"], d1["aup"], d0["k_a"],
      lw["r_k"], lw["gup"], lw["lnx_w"], lw["lnx_b"], lw["hm"], lw["w_out"])


def _pad_lora(w_up):
    zero = jnp.zeros_like(w_up[0])
    return (jnp.concatenate([w_up[0], zero], axis=0).astype(BF16),
            jnp.concatenate([zero, w_up[1]], axis=0).astype(BF16))


def _layer_weights(w, i):
    row = lambda x: x.reshape(1, -1)
    wup = _pad_lora(w["rw_w_up"][i])
    aup = _pad_lora(w["rw_a_up"][i])
    lane = jnp.arange(MXU_DIM)
    hm = (lane[:, None] // HEAD_DIM == lane[None, :] // HEAD_DIM).astype(BF16)
    common = dict(k_k=row(w["rw_k_k"][i]), k_a=row(w["rw_k_a"][i]), hm=hm)
    dirs = [dict(common, w0=row(w["rw_w0"][i, d]), wup=wup[d], a0=row(w["rw_a0"][i, d]), aup=aup[d])
            for d in range(2)]
    w_in = w["w_in"][i].astype(BF16)
    return dict(
        ffn1=(row(w["ffn1_norm"][i]), w["ffn1_wg"][i].astype(BF16), w["ffn1_wu"][i].astype(BF16),
              w["ffn1_wd"][i].astype(BF16)),
        ffn2=(row(w["ffn2_norm"][i]), w["ffn2_wg"][i].astype(BF16), w["ffn2_wu"][i].astype(BF16),
              w["ffn2_wd"][i].astype(BF16)),
        mix_norm=row(w["mix_norm"][i]), w_qkv=w_in[:, :3 * NA_WIDTH], w_rw=w_in[:, 3 * NA_WIDTH:],
        mu=row(w["rw_mu"][i]), tz=_na_bias_table(w["na_rpb"][i]), dirs=dirs, hm=hm,
        r_k=row(w["rw_r_k"][i]), gup=w["rw_g_up"][i].astype(BF16),
        lnx_w=row(w["rw_lnx_w"][i]), lnx_b=row(w["rw_lnx_b"][i]), w_out=w["w_out"][i].astype(BF16),
        ple=(row(w["ple_norm"][i]), w["ple_gate"][i].astype(BF16), w["ple_up"][i].astype(BF16)),
    )


def _trunk(x, p, layers, final_norm):
    b, t, d = x.shape
    m = b * t
    h = x.reshape(m, d)
    depth = len(layers)
    p_rows = p.reshape(depth, m, -1)
    for i, lw in enumerate(layers):
        h = _ffn(h, *lw["ffn1"])
        qkv, um = _inproj(h, lw["mix_norm"], lw["w_qkv"], lw["w_rw"], lw["mu"], seq_len=t)
        yna = _na(qkv.reshape(b, t, -1), lw["tz"])
        um3 = um.reshape(b, t, -1)
        y0, y1 = _wkv_pair(um3, lw["dirs"][0], lw["dirs"][1])
        h = _outproj(h, yna.reshape(m, -1), y0.reshape(m, -1), y1.reshape(m, -1), um, lw)
        pn, pg, pu = lw["ple"]
        h = _ffn(h, *lw["ffn2"], ple=(p_rows, i, pn, pg, pu),
                 final_norm=final_norm if i == depth - 1 else None)
    return h.reshape(b, t, d)


def kernel(x_prompt, x_sample, p_prompt, p_sample, ffn1_norm, ffn1_wg, ffn1_wu, ffn1_wd, mix_norm, w_in, na_rpb, rw_mu, rw_w0, rw_w_up, rw_a0, rw_a_up, rw_g_up, rw_k_k, rw_k_a, rw_r_k, rw_lnx_w, rw_lnx_b, w_out, ffn2_norm, ffn2_wg, ffn2_wu, ffn2_wd, ple_norm, ple_gate, ple_up, final_norm):
    w = dict(ffn1_norm=ffn1_norm, ffn1_wg=ffn1_wg, ffn1_wu=ffn1_wu, ffn1_wd=ffn1_wd, mix_norm=mix_norm,
             w_in=w_in, na_rpb=na_rpb, rw_mu=rw_mu, rw_w0=rw_w0, rw_w_up=rw_w_up, rw_a0=rw_a0,
             rw_a_up=rw_a_up, rw_g_up=rw_g_up, rw_k_k=rw_k_k, rw_k_a=rw_k_a, rw_r_k=rw_r_k,
             rw_lnx_w=rw_lnx_w, rw_lnx_b=rw_lnx_b, w_out=w_out, ffn2_norm=ffn2_norm, ffn2_wg=ffn2_wg,
             ffn2_wu=ffn2_wu, ffn2_wd=ffn2_wd, ple_norm=ple_norm, ple_gate=ple_gate, ple_up=ple_up)
    layers = [_layer_weights(w, i) for i in range(ffn1_wg.shape[0])]
    fn = final_norm.reshape(1, -1)
    return (_trunk(x_prompt, p_prompt, layers, fn), _trunk(x_sample, p_sample, layers, fn))
```

```python
import functools

import jax
import jax.numpy as jnp
from jax import lax
from jax.experimental import pallas as pl
from jax.experimental.pallas import tpu as pltpu

F32 = jnp.float32
BF16 = jnp.bfloat16

GRID_W = 64
HEAD_DIM = 64
NA_HEADS = 8
RW_HEADS = 8
NA_WIDTH = NA_HEADS * HEAD_DIM
RW_WIDTH = RW_HEADS * HEAD_DIM
WIN_ROWS = 8
WIN_COLS = 16
DECAY_LORA = 64
AAA_LORA = 64
GATE_LORA = 128
RW_COLS = 3 * RW_WIDTH + 2 * DECAY_LORA + 2 * AAA_LORA + GATE_LORA
NORM_EPS = 1e-6
LNX_EPS = 64e-5
DECAY_SCALE = 0.606531
MASK_VALUE = -1e30

LANES = 128
MXU_DIM = 256
VMEM_LIMIT_BYTES = 52 * 1024 * 1024

ROW_TILE = 512
FFN_SUBTILES = 2
NA_ROWS = 8
NA_UNROLL = 4
CHUNK = 64
WKV_STEP = 512
GROUP_HEADS = MXU_DIM // HEAD_DIM


def _params(*sem):
    return pltpu.CompilerParams(dimension_semantics=sem, vmem_limit_bytes=VMEM_LIMIT_BYTES)


def _bdot(a, b):
    return jnp.dot(a.astype(BF16), b.astype(BF16), preferred_element_type=F32)


def _split(x, n):
    parts = []
    rem = x
    for _ in range(n):
        p = rem.astype(BF16)
        parts.append(p)
        rem = rem - p.astype(F32)
    return parts


def _rms(x, g):
    ms = jnp.mean(x * x, axis=-1, keepdims=True)
    return x * lax.rsqrt(ms + NORM_EPS) * g


def _sigmoid(x):
    return 1.0 / (1.0 + jnp.exp(-x))


def _ffn_kernel(*refs, with_ple, with_final):
    if with_ple:
        x_ref, g_ref, wg_ref, wu_ref, wd_ref, p_ref, pn_ref, pg_ref, pu_ref, fn_ref, o_ref = refs
    else:
        x_ref, g_ref, wg_ref, wu_ref, wd_ref, o_ref = refs
    sub = x_ref.shape[0] // FFN_SUBTILES
    rows = [slice(i * sub, (i + 1) * sub) for i in range(FFN_SUBTILES)]
    xns = [_rms(x_ref[r], g_ref[...]).astype(BF16) for r in rows]
    gates = [jnp.dot(xn, wg_ref[...], preferred_element_type=F32) for xn in xns]
    ups = [jnp.dot(xn, wu_ref[...], preferred_element_type=F32) for xn in xns]
    mids = [((a * _sigmoid(a)) * b).astype(BF16) for a, b in zip(gates, ups)]
    downs = [jnp.dot(m, wd_ref[...], preferred_element_type=F32) for m in mids]
    hs = [x_ref[r] + 0.5 * d for r, d in zip(rows, downs)]
    if with_ple:
        pgate = [_sigmoid(_bdot(_rms(h, pn_ref[...]), pg_ref[...])) for h in hs]
        pup = [_bdot(p_ref[r], pu_ref[...]) for r in rows]
        hs = [h + gt * up for h, gt, up in zip(hs, pgate, pup)]
        if with_final:
            hs = [_rms(h, fn_ref[...]) for h in hs]
    for r, h in zip(rows, hs):
        o_ref[r] = h


def _ffn(x, g, wg, wu, wd, ple=None, final_norm=None):
    m, d = x.shape
    dff = wg.shape[1]
    tm = min(ROW_TILE, m)
    row = lambda i: (i, 0)
    const = lambda i: (0, 0)
    resident = lambda shape: pl.BlockSpec(shape, const, pipeline_mode=pl.Buffered(1))
    in_specs = [pl.BlockSpec((tm, d), row), pl.BlockSpec((1, d), const),
                resident((d, dff)), resident((d, dff)), resident((dff, d))]
    args = [x, g, wg, wu, wd]
    if ple is not None:
        p, layer, pn, pg, pu = ple
        fn = final_norm if final_norm is not None else pn
        in_specs += [pl.BlockSpec((None, tm, p.shape[2]), lambda i: (layer, i, 0)),
                     pl.BlockSpec((1, d), const),
                     resident((d, d)), resident((p.shape[2], d)), pl.BlockSpec((1, d), const)]
        args += [p, pn, pg, pu, fn]
    kern = functools.partial(_ffn_kernel, with_ple=ple is not None, with_final=final_norm is not None)
    return pl.pallas_call(
        kern,
        grid=(m // tm,),
        in_specs=in_specs,
        out_specs=pl.BlockSpec((tm, d), row),
        out_shape=jax.ShapeDtypeStruct((m, d), F32),
        compiler_params=_params("parallel"),
        name="ffn",
    )(*args)


def _inproj_kernel(x_ref, xp_ref, xn_ref, g_ref, wq_ref, wr_ref, mu_ref, qkv_ref, um_ref, *, seq_len):
    i = pl.program_id(0)
    tm = x_ref.shape[0]
    half = tm // 2
    g = g_ref[...]
    lane = lax.broadcasted_iota(jnp.int32, (1, 3 * NA_WIDTH), 1)
    scale = jnp.where(lane < NA_WIDTH, HEAD_DIM ** -0.5, 1.0)
    has_prev = (i * tm) % seq_len != 0
    has_next = ((i + 1) * tm) % seq_len != 0
    xn = [_rms(x_ref[0:half], g).astype(BF16), _rms(x_ref[half:tm], g).astype(BF16)]
    before = [_rms(xp_ref[...], g).astype(BF16), _rms(x_ref[half - 8:half], g).astype(BF16)]
    after = [_rms(x_ref[half:half + 8], g).astype(BF16), _rms(xn_ref[...], g).astype(BF16)]
    keep_before = [has_prev, True]
    keep_after = [True, has_next]
    for s in range(2):
        rows = slice(s * half, (s + 1) * half)
        proj = jnp.dot(xn[s], wq_ref[...], preferred_element_type=F32)
        qkv_ref[rows] = (proj * scale).astype(BF16)
        ext = jnp.concatenate([before[s], xn[s], after[s]], axis=0)
        u = jnp.dot(ext, wr_ref[...], preferred_element_type=F32)
        u = jnp.concatenate([jnp.where(keep_before[s], u[0:8], 0.0), u[8:half + 8],
                             jnp.where(keep_after[s], u[half + 8:], 0.0)], axis=0)
        nbr = pltpu.roll(u, 1, axis=0) + pltpu.roll(u, half + 15, axis=0)
        uc = u[8:half + 8]
        um_ref[rows] = uc + mu_ref[...] * (0.5 * nbr[8:half + 8] - uc)


def _inproj(x, g, w_qkv, w_rw, mu, seq_len):
    m, d = x.shape
    tm = min(ROW_TILE, m, seq_len)
    na3 = 3 * NA_WIDTH
    hb = tm // 8
    nhb = m // 8
    const = lambda i: (0, 0)
    return pl.pallas_call(
        functools.partial(_inproj_kernel, seq_len=seq_len),
        grid=(m // tm,),
        in_specs=[
            pl.BlockSpec((tm, d), lambda i: (i, 0)),
            pl.BlockSpec((8, d), lambda i: (jnp.maximum(i * hb - 1, 0), 0)),
            pl.BlockSpec((8, d), lambda i: (jnp.minimum((i + 1) * hb, nhb - 1), 0)),
            pl.BlockSpec((1, d), const),
            pl.BlockSpec((d, na3), const),
            pl.BlockSpec((d, RW_COLS), const),
            pl.BlockSpec((1, RW_COLS), const),
        ],
        out_specs=[
            pl.BlockSpec((tm, na3), lambda i: (i, 0)),
            pl.BlockSpec((tm, RW_COLS), lambda i: (i, 0)),
        ],
        out_shape=[jax.ShapeDtypeStruct((m, na3), BF16), jax.ShapeDtypeStruct((m, RW_COLS), F32)],
        compiler_params=_params("parallel"),
        name="inproj",
    )(x, x, x, g, w_qkv, w_rw, mu)


def _na_bias_table(rpb):
    j = jnp.arange(GRID_W)
    c = jnp.arange(GRID_W)
    cs = jnp.clip(j - WIN_COLS // 2, 0, GRID_W - WIN_COLS)
    valid = (c[None, :] >= cs[:, None]) & (c[None, :] < cs[:, None] + WIN_COLS)
    cidx = jnp.clip(c[None, :] - j[:, None] + (WIN_COLS - 1), 0, 2 * WIN_COLS - 2)
    full = jnp.where(valid, rpb[:, :, cidx], MASK_VALUE)
    nri = 2 * WIN_ROWS - 2
    two = jnp.concatenate([full[:, 0:nri], full[:, 1:nri + 1]], axis=-1)
    two = two.reshape(NA_HEADS // 2, 2, nri, GRID_W, 2 * GRID_W)
    return jnp.transpose(two, (0, 2, 1, 3, 4)).reshape(NA_HEADS // 2, nri, 2 * GRID_W, 2 * GRID_W)


def _na_window_start(qb, rows):
    return jnp.clip(qb * NA_ROWS - WIN_ROWS // 2, 0, rows - (NA_ROWS + WIN_ROWS))


def _na_kernel(q_ref, k_ref, v_ref, tz_ref, o_ref, *, rows):
    qb = pl.program_id(1)
    first_row = _na_window_start(qb, rows)
    lane = lax.broadcasted_iota(jnp.int32, (GRID_W, 2 * HEAD_DIM), 1)
    low = lane < HEAD_DIM
    win = WIN_ROWS * GRID_W

    npair = NA_HEADS // 2
    sls = [slice(hp * 2 * HEAD_DIM, (hp + 1) * 2 * HEAD_DIM) for hp in range(npair)]
    zero = jnp.zeros((GRID_W, 2 * HEAD_DIM), BF16)

    def body(it, carry):
        qoff, koff, delta = [], [], []
        for n in range(NA_UNROLL):
            il = it * NA_UNROLL + n
            i = qb * NA_ROWS + il
            rs = jnp.clip(i - WIN_ROWS // 2, 0, rows - WIN_ROWS)
            delta.append(i - rs)
            koff.append(pl.multiple_of((rs - first_row) * GRID_W, GRID_W))
            qoff.append(pl.multiple_of(il * GRID_W, GRID_W))
        chains = [(n, hp) for n in range(NA_UNROLL) for hp in range(npair)]
        qps = [q_ref[pl.ds(qoff[n], GRID_W), sls[hp]] for n, hp in chains]
        qsts = [jnp.concatenate([jnp.where(low, qp, zero), jnp.where(low, zero, qp)], axis=0)
                for qp in qps]
        ss = [lax.dot_general(qsts[ci], k_ref[0, pl.ds(koff[n], win), sls[hp]], (((1,), (1,)), ((), ())),
                              preferred_element_type=F32)
              for ci, (n, hp) in enumerate(chains)]
        ss = [ss[ci] + jnp.concatenate(
            [tz_ref[hp, 2 * p - delta[n] + (WIN_ROWS - 1)] for p in range(WIN_ROWS // 2)], axis=1)
            for ci, (n, hp) in enumerate(chains)]
        ps = [jnp.exp(s - jnp.max(s, axis=1, keepdims=True)) for s in ss]
        ps = [p * (1.0 / jnp.sum(p, axis=1, keepdims=True)) for p in ps]
        os_ = [jnp.dot(ps[ci].astype(BF16), v_ref[0, pl.ds(koff[n], win), sls[hp]],
                       preferred_element_type=F32) for ci, (n, hp) in enumerate(chains)]
        outs = [jnp.where(low, o[0:GRID_W], o[GRID_W:2 * GRID_W]) for o in os_]
        for n in range(NA_UNROLL):
            o_ref[pl.ds(qoff[n], GRID_W), :] = jnp.concatenate(
                outs[n * npair:(n + 1) * npair], axis=1).astype(BF16)
        return carry

    lax.fori_loop(0, NA_ROWS // NA_UNROLL, body, 0)


def _na(qkv, tz):
    b, t, _ = qkv.shape
    rows = t // GRID_W
    nqb = rows // NA_ROWS
    blk = NA_ROWS * GRID_W
    kv_spec = lambda col: pl.BlockSpec(
        (pl.Element(1), pl.Element((NA_ROWS + WIN_ROWS) * GRID_W), pl.Element(NA_WIDTH)),
        lambda bi, qi: (bi, _na_window_start(qi, rows) * GRID_W, col * NA_WIDTH))
    return pl.pallas_call(
        functools.partial(_na_kernel, rows=rows),
        grid=(b, nqb),
        in_specs=[pl.BlockSpec((None, blk, NA_WIDTH), lambda bi, qi: (bi, qi, 0)), kv_spec(1), kv_spec(2),
                  pl.BlockSpec(tz.shape, lambda bi, qi: (0, 0, 0, 0))],
        out_specs=pl.BlockSpec((None, blk, NA_WIDTH), lambda bi, qi: (bi, qi, 0)),
        out_shape=jax.ShapeDtypeStruct((b, t, NA_WIDTH), BF16),
        compiler_params=_params("parallel", "arbitrary"),
        name="natten",
    )(qkv, qkv, qkv, tz)


def _head_sum(x, hm):
    return jnp.concatenate([_bdot(x[:, q * MXU_DIM:(q + 1) * MXU_DIM], hm)
                            for q in range(RW_WIDTH // MXU_DIM)], axis=1)


def _stack(x, low):
    xb = x.astype(BF16)
    zero = jnp.zeros((x.shape[0], LANES), BF16)
    blocks = []
    for h in range(GROUP_HEADS):
        tile = xb[:, (h // 2) * LANES:(h // 2 + 1) * LANES]
        kept = jnp.where(low, tile, zero) if h % 2 == 0 else jnp.where(low, zero, tile)
        blocks.append(jnp.concatenate([kept, zero] if h < 2 else [zero, kept], axis=1))
    return jnp.concatenate(blocks, axis=0)


def _head_transpose_pair(x0, x1, low):
    xt = jnp.concatenate([x0, x1], axis=0).T
    b = [xt[h * HEAD_DIM:(h + 1) * HEAD_DIM] for h in range(GROUP_HEADS)]
    rolled = [pltpu.roll(bh, HEAD_DIM, axis=1) for bh in b]
    y0 = jnp.concatenate([jnp.where(low, b[0], rolled[1]), jnp.where(low, b[2], rolled[3])], axis=1)
    y1 = jnp.concatenate([jnp.where(low, rolled[0], b[1]), jnp.where(low, rolled[2], b[3])], axis=1)
    return y0, y1


def _wkv_program(g, r_ref, k_ref, v_ref, wl_ref, al_ref, w0_ref, wup_ref, a0_ref, aup_ref,
                 kk_ref, ka_ref, hm_ref, y_ref, z_ref, rp_ref, yh_ref, ql_ref, *, reverse, nsteps):
    ts = r_ref.shape[0]
    nchunk = ts // CHUNK
    ngroup = RW_WIDTH // MXU_DIM

    ti = lax.broadcasted_iota(jnp.int32, (CHUNK, CHUNK), 0)
    si = lax.broadcasted_iota(jnp.int32, (CHUNK, CHUNK), 1)
    tri = jnp.where((si >= ti) if reverse else (si <= ti), 1.0, 0.0).astype(BF16)
    lane = lax.broadcasted_iota(jnp.int32, (CHUNK, MXU_DIM), 1)
    trow = lax.broadcasted_iota(jnp.int32, (CHUNK, MXU_DIM), 0)
    s_of_lane = lane % HEAD_DIM
    if reverse:
        strict, incl = s_of_lane > trow, s_of_lane >= trow
    else:
        strict, incl = s_of_lane < trow, s_of_lane <= trow
    eye_lc = jnp.where(s_of_lane == trow, 1.0, 0.0)
    low = lax.broadcasted_iota(jnp.int32, (CHUNK, LANES), 1) < HEAD_DIM

    chains = [(c, q) for c in range(nchunk) for q in range(ngroup)]
    lanes = [slice(q * MXU_DIM, (q + 1) * MXU_DIM) for q in range(ngroup)]
    rows = [slice(c * CHUNK, (c + 1) * CHUNK) for c in range(nchunk)]
    st = lambda x: _stack(x, low)

    starts_sequence = (g + nsteps - 1) % nsteps == 0
    zs = [jnp.where(starts_sequence, 0.0, z_ref[q]) for q in range(ngroup)]
    pending = list(range(nchunk - 1, -1, -1) if reverse else range(nchunk))

    def scan_one_chunk():
        if not pending:
            return
        c = pending.pop(0)
        ys = []
        for q in range(ngroup):
            both = jnp.dot(rp_ref[c, q], st(zs[q]), preferred_element_type=F32)
            ys.append(both[0:CHUNK] + yh_ref[c, q])
            zs[q] = both[CHUNK:] + ql_ref[c, q]
        y_ref[c * CHUNK:(c + 1) * CHUNK, :] = jnp.concatenate(ys, axis=1)

    scan_one_chunk()
    r = r_ref[...]
    k = k_ref[...]
    v = v_ref[...]
    logw = -DECAY_SCALE * _sigmoid(w0_ref[...] + _bdot(jnp.tanh(wl_ref[...]), wup_ref[...]))
    a = _sigmoid(a0_ref[...] + _bdot(al_ref[...], aup_ref[...]))
    scan_one_chunk()
    yield "prep"
    kk = k * kk_ref[...]
    kk = kk / jnp.maximum(jnp.sqrt(_head_sum(kk * kk, hm_ref[...])), 1e-12)
    kd = k * (1.0 + (a - 1.0) * ka_ref[...])
    bb = kk * a
    scan_one_chunk()
    yield "prep"
    cums = [sum(jnp.dot(tri, p, preferred_element_type=F32) for p in _split(logw[rows[c]], 3))
            for c in range(nchunk)]
    scan_one_chunk()
    yield "prep"
    dec = []
    for c in range(nchunk):
        cum, lw = cums[c], logw[rows[c]]
        cmid = cum[CHUNK // 2:CHUNK // 2 + 1]
        ctot = cum[0:1] if reverse else cum[CHUNK - 1:CHUNK]
        e1 = jnp.exp(cum - cmid)
        e1i = jnp.exp(cmid - cum)
        ee = jnp.exp(ctot - cum)
        emid = jnp.exp(cmid)
        r_g = r[rows[c]] * e1
        a_g = -kk[rows[c]] * (e1 * jnp.exp(-lw))
        dec.append(dict(
            r_g=r_g, a_g=a_g, b_g=bb[rows[c]] * e1i, k_g=kd[rows[c]] * e1i, r_s=r_g * emid,
            a_s=a_g * emid, b_e=bb[rows[c]] * ee, k_e=kd[rows[c]] * ee, v=v[rows[c]],
            wtot=jnp.exp(ctot)))
        yield "prep"

    tr = {}
    for name in ("b_g", "k_g", "b_e", "k_e"):
        for c in range(0, nchunk, 2):
            for q in range(ngroup):
                tr[name, c, q], tr[name, c + 1, q] = _head_transpose_pair(
                    dec[c][name][:, lanes[q]], dec[c + 1][name][:, lanes[q]], low)
        yield "prep"
    yield "prep_done"

    st_v = [st(dec[c]["v"][:, lanes[q]]) for c, q in chains]
    grams = [_bdot(jnp.concatenate([dec[c]["a_g"][:, lanes[q]], dec[c]["r_g"][:, lanes[q]]], axis=0),
                   jnp.concatenate([st(tr["b_g", c, q]), st(tr["k_g", c, q])], axis=1))
             for c, q in chains]
    scan_one_chunk()
    yield "stage"
    a_ab = [jnp.where(strict, g_[0:CHUNK, 0:MXU_DIM], 0.0) for g_ in grams]
    a_kr = [jnp.concatenate([jnp.where(strict, g_[0:CHUNK, MXU_DIM:], 0.0),
                             jnp.where(incl, g_[CHUNK:, MXU_DIM:], 0.0)], axis=0) for g_ in grams]
    a_rb = [jnp.where(incl, g_[CHUNK:, 0:MXU_DIM], 0.0) for g_ in grams]
    tinv = [eye_lc + a_ for a_ in a_ab]
    apow = [_bdot(a_, st(a_)) for a_ in a_ab]
    scan_one_chunk()
    yield "stage"
    npow = CHUNK.bit_length() - 2
    for jj in range(npow):
        if jj < npow - 1:
            both = [_bdot(jnp.concatenate([ap, ti_], axis=0), st(ap)) for ap, ti_ in zip(apow, tinv)]
            apow = [b_[0:CHUNK] for b_ in both]
            tinv = [ti_ + b_[CHUNK:] for ti_, b_ in zip(tinv, both)]
        else:
            tinv = [ti_ + _bdot(ti_, st(ap)) for ap, ti_ in zip(apow, tinv)]
        scan_one_chunk()
        yield "stage"
    xyq = [_bdot(jnp.concatenate([a_kr[i], tr["k_e", c, q]], axis=0), st_v[i])
           for i, (c, q) in enumerate(chains)]
    xy = [x[0:2 * CHUNK] for x in xyq]
    q2 = [x[2 * CHUNK:] for x in xyq]
    yield "stage"
    au = [_bdot(tinv[i], jnp.concatenate([st(dec[c]["a_s"][:, lanes[q]]), st(xy[i][0:CHUNK])], axis=1))
          for i, (c, q) in enumerate(chains)]
    while pending:
        scan_one_chunk()
    yield "stage"
    st_au = [jnp.concatenate([st(x[:, 0:MXU_DIM]), st(x[:, MXU_DIM:])], axis=1) for x in au]
    rypq = [_bdot(jnp.concatenate([a_rb[i], tr["b_e", c, q]], axis=0), st_au[i])
            for i, (c, q) in enumerate(chains)]
    ry = [x[0:CHUNK] for x in rypq]
    pq = [x[CHUNK:] for x in rypq]
    for q in range(ngroup):
        z_ref[q] = zs[q]
    for i, (c, q) in enumerate(chains):
        r_hat = dec[c]["r_s"][:, lanes[q]] + ry[i][:, 0:MXU_DIM]
        p_lc = eye_lc * dec[c]["wtot"][:, lanes[q]] + pq[i][:, 0:MXU_DIM]
        rp_ref[c, q] = jnp.concatenate([r_hat, p_lc], axis=0).astype(BF16)
        yh_ref[c, q] = ry[i][:, MXU_DIM:] + xy[i][CHUNK:]
        ql_ref[c, q] = pq[i][:, MXU_DIM:] + q2[i]
    yield "stage"


_WKV_DIR_INPUTS = 9
_WKV_DIR_SCRATCH = 4


def _wkv_pair_kernel(*refs, nsteps):
    ni, ns = _WKV_DIR_INPUTS, _WKV_DIR_SCRATCH
    fwd_in, bwd_in, shared = refs[0:ni], refs[ni:2 * ni], refs[2 * ni:2 * ni + 3]
    y_fwd, y_bwd = refs[2 * ni + 3:2 * ni + 5]
    scratch = refs[2 * ni + 5:]
    g = pl.program_id(0)

    @pl.when(g == 0)
    def _():
        for ref in scratch:
            ref[...] = jnp.zeros_like(ref)

    fwd = _wkv_program(g, *fwd_in, *shared, y_fwd, *scratch[0:ns], reverse=False, nsteps=nsteps)
    bwd = _wkv_program(g, *bwd_in, *shared, y_bwd, *scratch[ns:2 * ns], reverse=True, nsteps=nsteps)
    for tag in fwd:
        if tag == "prep_done":
            break
    bwd_preparing = True
    for _ in fwd:
        if bwd_preparing:
            bwd_preparing = next(bwd) != "prep_done"
    while bwd_preparing:
        bwd_preparing = next(bwd) != "prep_done"
    for _ in bwd:
        pass


_UM_WL_BLOCK = 3 * RW_WIDTH // LANES
_UM_AL_BLOCK = _UM_WL_BLOCK + 1
_UM_GL_BLOCK = _UM_WL_BLOCK + 2


def _wkv_pair(um, d_fwd, d_bwd):
    b, t, _ = um.shape
    ts = min(WKV_STEP, t)
    nsteps = t // ts
    nblocks = b * nsteps
    nchunk = ts // CHUNK
    ngroup = RW_WIDTH // MXU_DIM
    const = lambda g: (0, 0)
    vec = pl.BlockSpec((1, RW_WIDTH), const)

    def dir_specs(reverse):
        pos = (lambda si: nsteps - 1 - si) if reverse else (lambda si: si)
        block_of = lambda gi, col: (gi // nsteps, pos(gi % nsteps), col)
        wide = lambda col: pl.BlockSpec((None, ts, RW_WIDTH),
                                        lambda g: block_of(jnp.minimum(g, nblocks - 1), col))
        narrow = lambda col: pl.BlockSpec((None, ts, LANES),
                                          lambda g: block_of(jnp.minimum(g, nblocks - 1), col))
        ins = [wide(0), wide(1), wide(2), narrow(_UM_WL_BLOCK), narrow(_UM_AL_BLOCK), vec,
               pl.BlockSpec((2 * DECAY_LORA, RW_WIDTH), const), vec,
               pl.BlockSpec((2 * AAA_LORA, RW_WIDTH), const)]
        out = pl.BlockSpec((None, ts, RW_WIDTH), lambda g: block_of(jnp.maximum(g - 1, 0), 0))
        return ins, out

    dir_args = lambda wp: [um, um, um, um, um, wp["w0"], wp["wup"], wp["a0"], wp["aup"]]
    fwd_ins, fwd_out = dir_specs(False)
    bwd_ins, bwd_out = dir_specs(True)
    dir_scratch = [pltpu.VMEM((ngroup, HEAD_DIM, MXU_DIM), F32),
                   pltpu.VMEM((nchunk, ngroup, 2 * CHUNK, MXU_DIM), BF16),
                   pltpu.VMEM((nchunk, ngroup, CHUNK, MXU_DIM), F32),
                   pltpu.VMEM((nchunk, ngroup, HEAD_DIM, MXU_DIM), F32)]
    shp = jax.ShapeDtypeStruct((b, t, RW_WIDTH), F32)
    return pl.pallas_call(
        functools.partial(_wkv_pair_kernel, nsteps=nsteps),
        grid=(nblocks + 1,),
        in_specs=fwd_ins + bwd_ins + [vec, vec, pl.BlockSpec((MXU_DIM, MXU_DIM), const)],
        out_specs=[fwd_out, bwd_out],
        out_shape=[shp, shp],
        scratch_shapes=dir_scratch + dir_scratch,
        compiler_params=_params("arbitrary"),
        name="wkv",
    )(*dir_args(d_fwd), *dir_args(d_bwd), d_fwd["k_k"], d_fwd["k_a"], d_fwd["hm"])


def _outproj_kernel(h_ref, yna_ref, y0_ref, y1_ref, r_ref, k_ref, v_ref, al_ref, gl_ref,
                    a00_ref, aup0_ref, a01_ref, aup1_ref, ka_ref, rk_ref, gup_ref, lw_ref, lb_ref,
                    hm_ref, w_ref, o_ref):
    hm = hm_ref[...]
    half = h_ref.shape[0] // 2
    groups = [slice(0, half), slice(half, 2 * half)]
    wkv = [y0_ref[rows] + y1_ref[rows] for rows in groups]
    mean = [_head_sum(x, hm) * (1.0 / HEAD_DIM) for x in wkv]
    dev = [x - m_ for x, m_ in zip(wkv, mean)]
    var = [_head_sum(d_ * d_, hm) * (1.0 / HEAD_DIM) for d_ in dev]
    yn = [d_ * lax.rsqrt(v_ + LNX_EPS) * lw_ref[...] + lb_ref[...] for d_, v_ in zip(dev, var)]
    a_sum = [_sigmoid(a00_ref[...] + _bdot(al_ref[rows], aup0_ref[...]))
             + _sigmoid(a01_ref[...] + _bdot(al_ref[rows], aup1_ref[...])) for rows in groups]
    kd_sum = [k_ref[rows] * (2.0 + (a_ - 2.0) * ka_ref[...]) for rows, a_ in zip(groups, a_sum)]
    bonus = [_head_sum(r_ref[rows] * kd * rk_ref[...], hm) * v_ref[rows] for rows, kd in zip(groups, kd_sum)]
    gate = [_bdot(_sigmoid(gl_ref[rows]), gup_ref[...]) for rows in groups]
    for rows, yn_, bonus_, gate_ in zip(groups, yn, bonus, gate):
        y_rw = (yn_ + bonus_) * gate_
        mix = jnp.concatenate([yna_ref[rows], y_rw.astype(BF16)], axis=1)
        o_ref[rows] = h_ref[rows] + jnp.dot(mix, w_ref[...], preferred_element_type=F32)


def _outproj(h, yna, y0, y1, um, lw):
    m, d = h.shape
    tm = min(ROW_TILE, m)
    row = lambda i: (i, 0)
    const = lambda i: (0, 0)
    tok = pl.BlockSpec((tm, RW_WIDTH), row)
    vec = pl.BlockSpec((1, RW_WIDTH), const)
    um_wide = lambda col: pl.BlockSpec((tm, RW_WIDTH), lambda i: (i, col))
    um_narrow = lambda col: pl.BlockSpec((tm, LANES), lambda i: (i, col))
    lora = pl.BlockSpec((2 * AAA_LORA, RW_WIDTH), const)
    d0, d1 = lw["dirs"]
    return pl.pallas_call(
        _outproj_kernel,
        grid=(m // tm,),
        in_specs=[pl.BlockSpec((tm, d), row), pl.BlockSpec((tm, NA_WIDTH), row), tok, tok,
                  um_wide(0), um_wide(1), um_wide(2), um_narrow(_UM_AL_BLOCK), um_narrow(_UM_GL_BLOCK),
                  vec, lora, vec, lora, vec, vec, pl.BlockSpec((GATE_LORA, RW_WIDTH), const), vec, vec,
                  pl.BlockSpec((MXU_DIM, MXU_DIM), const),
                  pl.BlockSpec((NA_WIDTH + RW_WIDTH, d), const)],
        out_specs=pl.BlockSpec((tm, d), row),
        out_shape=jax.ShapeDtypeStruct((m, d), F32),
        compiler_params=_params("parallel"),
        name="outproj",
    )(h, yna, y0, y1, um, um, um, um, um, d0["a0"], d0["aup"], d1["a0"], d1["aup"], d0["k_a"],
      lw["r_k"], lw["gup"], lw["lnx_w"], lw["lnx_b"], lw["hm"], lw["w_out"])


def _pad_lora(w_up):
    zero = jnp.zeros_like(w_up[0])
    return (jnp.concatenate([w_up[0], zero], axis=0).astype(BF16),
            jnp.concatenate([zero, w_up[1]], axis=0).astype(BF16))


def _layer_weights(w, i):
    row = lambda x: x.reshape(1, -1)
    wup = _pad_lora(w["rw_w_up"][i])
    aup = _pad_lora(w["rw_a_up"][i])
    lane = jnp.arange(MXU_DIM)
    hm = (lane[:, None] // HEAD_DIM == lane[None, :] // HEAD_DIM).astype(BF16)
    common = dict(k_k=row(w["rw_k_k"][i]), k_a=row(w["rw_k_a"][i]), hm=hm)
    dirs = [dict(common, w0=row(w["rw_w0"][i, d]), wup=wup[d], a0=row(w["rw_a0"][i, d]), aup=aup[d])
            for d in range(2)]
    w_in = w["w_in"][i].astype(BF16)
    return dict(
        ffn1=(row(w["ffn1_norm"][i]), w["ffn1_wg"][i].astype(BF16), w["ffn1_wu"][i].astype(BF16),
              w["ffn1_wd"][i].astype(BF16)),
        ffn2=(row(w["ffn2_norm"][i]), w["ffn2_wg"][i].astype(BF16), w["ffn2_wu"][i].astype(BF16),
              w["ffn2_wd"][i].astype(BF16)),
        mix_norm=row(w["mix_norm"][i]), w_qkv=w_in[:, :3 * NA_WIDTH], w_rw=w_in[:, 3 * NA_WIDTH:],
        mu=row(w["rw_mu"][i]), tz=_na_bias_table(w["na_rpb"][i]), dirs=dirs, hm=hm,
        r_k=row(w["rw_r_k"][i]), gup=w["rw_g_up"][i].astype(BF16),
        lnx_w=row(w["rw_lnx_w"][i]), lnx_b=row(w["rw_lnx_b"][i]), w_out=w["w_out"][i].astype(BF16),
        ple=(row(w["ple_norm"][i]), w["ple_gate"][i].astype(BF16), w["ple_up"][i].astype(BF16)),
    )


def _trunk(x, p, layers, final_norm):
    b, t, d = x.shape
    m = b * t
    h = x.reshape(m, d)
    depth = len(layers)
    p_rows = p.reshape(depth, m, -1)
    for i, lw in enumerate(layers):
        h = _ffn(h, *lw["ffn1"])
        qkv, um = _inproj(h, lw["mix_norm"], lw["w_qkv"], lw["w_rw"], lw["mu"], seq_len=t)
        yna = _na(qkv.reshape(b, t, -1), lw["tz"])
        um3 = um.reshape(b, t, -1)
        y0, y1 = _wkv_pair(um3, lw["dirs"][0], lw["dirs"][1])
        h = _outproj(h, yna.reshape(m, -1), y0.reshape(m, -1), y1.reshape(m, -1), um, lw)
        pn, pg, pu = lw["ple"]
        h = _ffn(h, *lw["ffn2"], ple=(p_rows, i, pn, pg, pu),
                 final_norm=final_norm if i == depth - 1 else None)
    return h.reshape(b, t, d)


def kernel(x_prompt, x_sample, p_prompt, p_sample, ffn1_norm, ffn1_wg, ffn1_wu, ffn1_wd, mix_norm, w_in, na_rpb, rw_mu, rw_w0, rw_w_up, rw_a0, rw_a_up, rw_g_up, rw_k_k, rw_k_a, rw_r_k, rw_lnx_w, rw_lnx_b, w_out, ffn2_norm, ffn2_wg, ffn2_wu, ffn2_wd, ple_norm, ple_gate, ple_up, final_norm):
    w = dict(ffn1_norm=ffn1_norm, ffn1_wg=ffn1_wg, ffn1_wu=ffn1_wu, ffn1_wd=ffn1_wd, mix_norm=mix_norm,
             w_in=w_in, na_rpb=na_rpb, rw_mu=rw_mu, rw_w0=rw_w0, rw_w_up=rw_w_up, rw_a0=rw_a0,
             rw_a_up=rw_a_up, rw_g_up=rw_g_up, rw_k_k=rw_k_k, rw_k_a=rw_k_a, rw_r_k=rw_r_k,
             rw_lnx_w=rw_lnx_w, rw_lnx_b=rw_lnx_b, w_out=w_out, ffn2_norm=ffn2_norm, ffn2_wg=ffn2_wg,
             ffn2_wu=ffn2_wu, ffn2_wd=ffn2_wd, ple_norm=ple_norm, ple_gate=ple_gate, ple_up=ple_up)
    layers = [_layer_weights(w, i) for i in range(ffn1_wg.shape[0])]
    fn = final_norm.reshape(1, -1)
    return (_trunk(x_prompt, p_prompt, layers, fn), _trunk(x_sample, p_sample, layers, fn))
```

```python
import functools

import jax
import jax.numpy as jnp
from jax import lax
from jax.experimental import pallas as pl
from jax.experimental.pallas import tpu as pltpu

F32 = jnp.float32
BF16 = jnp.bfloat16

GRID_W = 64
HEAD_DIM = 64
NA_HEADS = 8
RW_HEADS = 8
NA_WIDTH = NA_HEADS * HEAD_DIM
RW_WIDTH = RW_HEADS * HEAD_DIM
WIN_ROWS = 8
WIN_COLS = 16
DECAY_LORA = 64
AAA_LORA = 64
GATE_LORA = 128
RW_COLS = 3 * RW_WIDTH + 2 * DECAY_LORA + 2 * AAA_LORA + GATE_LORA
NORM_EPS = 1e-6
LNX_EPS = 64e-5
DECAY_SCALE = 0.606531
MASK_VALUE = -1e30

LANES = 128
MXU_DIM = 256
VMEM_LIMIT_BYTES = 52 * 1024 * 1024

ROW_TILE = 512
FFN_SUBTILES = 2
NA_ROWS = 8
NA_UNROLL = 4
CHUNK = 64
WKV_STEP = 512
GROUP_HEADS = MXU_DIM // HEAD_DIM


def _params(*sem):
    return pltpu.CompilerParams(dimension_semantics=sem, vmem_limit_bytes=VMEM_LIMIT_BYTES)


def _bdot(a, b):
    return jnp.dot(a.astype(BF16), b.astype(BF16), preferred_element_type=F32)


def _split(x, n):
    parts = []
    rem = x
    for _ in range(n):
        p = rem.astype(BF16)
        parts.append(p)
        rem = rem - p.astype(F32)
    return parts


def _rms(x, g):
    ms = jnp.mean(x * x, axis=-1, keepdims=True)
    return x * lax.rsqrt(ms + NORM_EPS) * g


def _sigmoid(x):
    return 1.0 / (1.0 + jnp.exp(-x))


def _ffn_kernel(*refs, with_ple, with_final):
    if with_ple:
        x_ref, g_ref, wg_ref, wu_ref, wd_ref, p_ref, pn_ref, pg_ref, pu_ref, fn_ref, o_ref = refs
    else:
        x_ref, g_ref, wg_ref, wu_ref, wd_ref, o_ref = refs
    sub = x_ref.shape[0] // FFN_SUBTILES
    rows = [slice(i * sub, (i + 1) * sub) for i in range(FFN_SUBTILES)]
    xns = [_rms(x_ref[r], g_ref[...]).astype(BF16) for r in rows]
    gates = [jnp.dot(xn, wg_ref[...], preferred_element_type=F32) for xn in xns]
    ups = [jnp.dot(xn, wu_ref[...], preferred_element_type=F32) for xn in xns]
    mids = [((a * _sigmoid(a)) * b).astype(BF16) for a, b in zip(gates, ups)]
    downs = [jnp.dot(m, wd_ref[...], preferred_element_type=F32) for m in mids]
    hs = [x_ref[r] + 0.5 * d for r, d in zip(rows, downs)]
    if with_ple:
        pgate = [_sigmoid(_bdot(_rms(h, pn_ref[...]), pg_ref[...])) for h in hs]
        pup = [_bdot(p_ref[r], pu_ref[...]) for r in rows]
        hs = [h + gt * up for h, gt, up in zip(hs, pgate, pup)]
        if with_final:
            hs = [_rms(h, fn_ref[...]) for h in hs]
    for r, h in zip(rows, hs):
        o_ref[r] = h


def _ffn(x, g, wg, wu, wd, ple=None, final_norm=None):
    m, d = x.shape
    dff = wg.shape[1]
    tm = min(ROW_TILE, m)
    row = lambda i: (i, 0)
    const = lambda i: (0, 0)
    resident = lambda shape: pl.BlockSpec(shape, const, pipeline_mode=pl.Buffered(1))
    in_specs = [pl.BlockSpec((tm, d), row), pl.BlockSpec((1, d), const),
                resident((d, dff)), resident((d, dff)), resident((dff, d))]
    args = [x, g, wg, wu, wd]
    if ple is not None:
        p, layer, pn, pg, pu = ple
        fn = final_norm if final_norm is not None else pn
        in_specs += [pl.BlockSpec((None, tm, p.shape[2]), lambda i: (layer, i, 0)),
                     pl.BlockSpec((1, d), const),
                     resident((d, d)), resident((p.shape[2], d)), pl.BlockSpec((1, d), const)]
        args += [p, pn, pg, pu, fn]
    kern = functools.partial(_ffn_kernel, with_ple=ple is not None, with_final=final_norm is not None)
    return pl.pallas_call(
        kern,
        grid=(m // tm,),
        in_specs=in_specs,
        out_specs=pl.BlockSpec((tm, d), row),
        out_shape=jax.ShapeDtypeStruct((m, d), F32),
        compiler_params=_params("parallel"),
        name="ffn",
    )(*args)


def _inproj_kernel(x_ref, xp_ref, xn_ref, g_ref, wq_ref, wr_ref, mu_ref, qkv_ref, um_ref, *, seq_len):
    i = pl.program_id(0)
    tm = x_ref.shape[0]
    half = tm // 2
    g = g_ref[...]
    lane = lax.broadcasted_iota(jnp.int32, (1, 3 * NA_WIDTH), 1)
    scale = jnp.where(lane < NA_WIDTH, HEAD_DIM ** -0.5, 1.0)
    has_prev = (i * tm) % seq_len != 0
    has_next = ((i + 1) * tm) % seq_len != 0
    xn = [_rms(x_ref[0:half], g).astype(BF16), _rms(x_ref[half:tm], g).astype(BF16)]
    before = [_rms(xp_ref[...], g).astype(BF16), _rms(x_ref[half - 8:half], g).astype(BF16)]
    after = [_rms(x_ref[half:half + 8], g).astype(BF16), _rms(xn_ref[...], g).astype(BF16)]
    keep_before = [has_prev, True]
    keep_after = [True, has_next]
    for s in range(2):
        rows = slice(s * half, (s + 1) * half)
        proj = jnp.dot(xn[s], wq_ref[...], preferred_element_type=F32)
        qkv_ref[rows] = (proj * scale).astype(BF16)
        ext = jnp.concatenate([before[s], xn[s], after[s]], axis=0)
        u = jnp.dot(ext, wr_ref[...], preferred_element_type=F32)
        u = jnp.concatenate([jnp.where(keep_before[s], u[0:8], 0.0), u[8:half + 8],
                             jnp.where(keep_after[s], u[half + 8:], 0.0)], axis=0)
        nbr = pltpu.roll(u, 1, axis=0) + pltpu.roll(u, half + 15, axis=0)
        uc = u[8:half + 8]
        um_ref[rows] = uc + mu_ref[...] * (0.5 * nbr[8:half + 8] - uc)


def _inproj(x, g, w_qkv, w_rw, mu, seq_len):
    m, d = x.shape
    tm = min(ROW_TILE, m, seq_len)
    na3 = 3 * NA_WIDTH
    hb = tm // 8
    nhb = m // 8
    const = lambda i: (0, 0)
    return pl.pallas_call(
        functools.partial(_inproj_kernel, seq_len=seq_len),
        grid=(m // tm,),
        in_specs=[
            pl.BlockSpec((tm, d), lambda i: (i, 0)),
            pl.BlockSpec((8, d), lambda i: (jnp.maximum(i * hb - 1, 0), 0)),
            pl.BlockSpec((8, d), lambda i: (jnp.minimum((i + 1) * hb, nhb - 1), 0)),
            pl.BlockSpec((1, d), const),
            pl.BlockSpec((d, na3), const),
            pl.BlockSpec((d, RW_COLS), const),
            pl.BlockSpec((1, RW_COLS), const),
        ],
        out_specs=[
            pl.BlockSpec((tm, na3), lambda i: (i, 0)),
            pl.BlockSpec((tm, RW_COLS), lambda i: (i, 0)),
        ],
        out_shape=[jax.ShapeDtypeStruct((m, na3), BF16), jax.ShapeDtypeStruct((m, RW_COLS), F32)],
        compiler_params=_params("parallel"),
        name="inproj",
    )(x, x, x, g, w_qkv, w_rw, mu)


def _na_bias_table(rpb):
    j = jnp.arange(GRID_W)
    c = jnp.arange(GRID_W)
    cs = jnp.clip(j - WIN_COLS // 2, 0, GRID_W - WIN_COLS)
    valid = (c[None, :] >= cs[:, None]) & (c[None, :] < cs[:, None] + WIN_COLS)
    cidx = jnp.clip(c[None, :] - j[:, None] + (WIN_COLS - 1), 0, 2 * WIN_COLS - 2)
    full = jnp.where(valid, rpb[:, :, cidx], MASK_VALUE)
    nri = 2 * WIN_ROWS - 2
    two = jnp.concatenate([full[:, 0:nri], full[:, 1:nri + 1]], axis=-1)
    two = two.reshape(NA_HEADS // 2, 2, nri, GRID_W, 2 * GRID_W)
    return jnp.transpose(two, (0, 2, 1, 3, 4)).reshape(NA_HEADS // 2, nri, 2 * GRID_W, 2 * GRID_W)


def _na_window_start(qb, rows):
    return jnp.clip(qb * NA_ROWS - WIN_ROWS // 2, 0, rows - (NA_ROWS + WIN_ROWS))


def _na_kernel(q_ref, k_ref, v_ref, tz_ref, o_ref, *, rows):
    qb = pl.program_id(1)
    first_row = _na_window_start(qb, rows)
    lane = lax.broadcasted_iota(jnp.int32, (GRID_W, 2 * HEAD_DIM), 1)
    low = lane < HEAD_DIM
    win = WIN_ROWS * GRID_W

    npair = NA_HEADS // 2
    sls = [slice(hp * 2 * HEAD_DIM, (hp + 1) * 2 * HEAD_DIM) for hp in range(npair)]
    zero = jnp.zeros((GRID_W, 2 * HEAD_DIM), BF16)

    def body(it, carry):
        qoff, koff, delta = [], [], []
        for n in range(NA_UNROLL):
            il = it * NA_UNROLL + n
            i = qb * NA_ROWS + il
            rs = jnp.clip(i - WIN_ROWS // 2, 0, rows - WIN_ROWS)
            delta.append(i - rs)
            koff.append(pl.multiple_of((rs - first_row) * GRID_W, GRID_W))
            qoff.append(pl.multiple_of(il * GRID_W, GRID_W))
        chains = [(n, hp) for n in range(NA_UNROLL) for hp in range(npair)]
        qps = [q_ref[pl.ds(qoff[n], GRID_W), sls[hp]] for n, hp in chains]
        qsts = [jnp.concatenate([jnp.where(low, qp, zero), jnp.where(low, zero, qp)], axis=0)
                for qp in qps]
        ss = [lax.dot_general(qsts[ci], k_ref[0, pl.ds(koff[n], win), sls[hp]], (((1,), (1,)), ((), ())),
                              preferred_element_type=F32)
              for ci, (n, hp) in enumerate(chains)]
        ss = [ss[ci] + jnp.concatenate(
            [tz_ref[hp, 2 * p - delta[n] + (WIN_ROWS - 1)] for p in range(WIN_ROWS // 2)], axis=1)
            for ci, (n, hp) in enumerate(chains)]
        ps = [jnp.exp(s - jnp.max(s, axis=1, keepdims=True)) for s in ss]
        ps = [p * (1.0 / jnp.sum(p, axis=1, keepdims=True)) for p in ps]
        os_ = [jnp.dot(ps[ci].astype(BF16), v_ref[0, pl.ds(koff[n], win), sls[hp]],
                       preferred_element_type=F32) for ci, (n, hp) in enumerate(chains)]
        outs = [jnp.where(low, o[0:GRID_W], o[GRID_W:2 * GRID_W]) for o in os_]
        for n in range(NA_UNROLL):
            o_ref[pl.ds(qoff[n], GRID_W), :] = jnp.concatenate(
                outs[n * npair:(n + 1) * npair], axis=1).astype(BF16)
        return carry

    lax.fori_loop(0, NA_ROWS // NA_UNROLL, body, 0)


def _na(qkv, tz):
    b, t, _ = qkv.shape
    rows = t // GRID_W
    nqb = rows // NA_ROWS
    blk = NA_ROWS * GRID_W
    kv_spec = lambda col: pl.BlockSpec(
        (pl.Element(1), pl.Element((NA_ROWS + WIN_ROWS) * GRID_W), pl.Element(NA_WIDTH)),
        lambda bi, qi: (bi, _na_window_start(qi, rows) * GRID_W, col * NA_WIDTH))
    return pl.pallas_call(
        functools.partial(_na_kernel, rows=rows),
        grid=(b, nqb),
        in_specs=[pl.BlockSpec((None, blk, NA_WIDTH), lambda bi, qi: (bi, qi, 0)), kv_spec(1), kv_spec(2),
                  pl.BlockSpec(tz.shape, lambda bi, qi: (0, 0, 0, 0))],
        out_specs=pl.BlockSpec((None, blk, NA_WIDTH), lambda bi, qi: (bi, qi, 0)),
        out_shape=jax.ShapeDtypeStruct((b, t, NA_WIDTH), BF16),
        compiler_params=_params("parallel", "arbitrary"),
        name="natten",
    )(qkv, qkv, qkv, tz)


def _head_sum(x, hm):
    return jnp.concatenate([_bdot(x[:, q * MXU_DIM:(q + 1) * MXU_DIM], hm)
                            for q in range(RW_WIDTH // MXU_DIM)], axis=1)


def _stack(x, low):
    xb = x.astype(BF16)
    zero = jnp.zeros((x.shape[0], LANES), BF16)
    blocks = []
    for h in range(GROUP_HEADS):
        tile = xb[:, (h // 2) * LANES:(h // 2 + 1) * LANES]
        kept = jnp.where(low, tile, zero) if h % 2 == 0 else jnp.where(low, zero, tile)
        blocks.append(jnp.concatenate([kept, zero] if h < 2 else [zero, kept], axis=1))
    return jnp.concatenate(blocks, axis=0)


def _head_transpose_pair(x0, x1, low):
    xt = jnp.concatenate([x0, x1], axis=0).T
    b = [xt[h * HEAD_DIM:(h + 1) * HEAD_DIM] for h in range(GROUP_HEADS)]
    rolled = [pltpu.roll(bh, HEAD_DIM, axis=1) for bh in b]
    y0 = jnp.concatenate([jnp.where(low, b[0], rolled[1]), jnp.where(low, b[2], rolled[3])], axis=1)
    y1 = jnp.concatenate([jnp.where(low, rolled[0], b[1]), jnp.where(low, rolled[2], b[3])], axis=1)
    return y0, y1


def _wkv_program(g, r_ref, k_ref, v_ref, wl_ref, al_ref, w0_ref, wup_ref, a0_ref, aup_ref,
                 kk_ref, ka_ref, hm_ref, y_ref, z_ref, rp_ref, yh_ref, ql_ref, *, reverse, nsteps):
    ts = r_ref.shape[0]
    nchunk = ts // CHUNK
    ngroup = RW_WIDTH // MXU_DIM

    ti = lax.broadcasted_iota(jnp.int32, (CHUNK, CHUNK), 0)
    si = lax.broadcasted_iota(jnp.int32, (CHUNK, CHUNK), 1)
    tri = jnp.where((si >= ti) if reverse else (si <= ti), 1.0, 0.0).astype(BF16)
    lane = lax.broadcasted_iota(jnp.int32, (CHUNK, MXU_DIM), 1)
    trow = lax.broadcasted_iota(jnp.int32, (CHUNK, MXU_DIM), 0)
    s_of_lane = lane % HEAD_DIM
    if reverse:
        strict, incl = s_of_lane > trow, s_of_lane >= trow
    else:
        strict, incl = s_of_lane < trow, s_of_lane <= trow
    eye_lc = jnp.where(s_of_lane == trow, 1.0, 0.0)
    low = lax.broadcasted_iota(jnp.int32, (CHUNK, LANES), 1) < HEAD_DIM

    chains = [(c, q) for c in range(nchunk) for q in range(ngroup)]
    lanes = [slice(q * MXU_DIM, (q + 1) * MXU_DIM) for q in range(ngroup)]
    rows = [slice(c * CHUNK, (c + 1) * CHUNK) for c in range(nchunk)]
    st = lambda x: _stack(x, low)

    starts_sequence = (g + nsteps - 1) % nsteps == 0
    zs = [jnp.where(starts_sequence, 0.0, z_ref[q]) for q in range(ngroup)]
    pending = list(range(nchunk - 1, -1, -1) if reverse else range(nchunk))

    def scan_one_chunk():
        if not pending:
            return
        c = pending.pop(0)
        ys = []
        for q in range(ngroup):
            both = jnp.dot(rp_ref[c, q], st(zs[q]), preferred_element_type=F32)
            ys.append(both[0:CHUNK] + yh_ref[c, q])
            zs[q] = both[CHUNK:] + ql_ref[c, q]
        y_ref[c * CHUNK:(c + 1) * CHUNK, :] = jnp.concatenate(ys, axis=1)

    scan_one_chunk()
    r = r_ref[...]
    k = k_ref[...]
    v = v_ref[...]
    logw = -DECAY_SCALE * _sigmoid(w0_ref[...] + _bdot(jnp.tanh(wl_ref[...]), wup_ref[...]))
    a = _sigmoid(a0_ref[...] + _bdot(al_ref[...], aup_ref[...]))
    scan_one_chunk()
    yield "prep"
    kk = k * kk_ref[...]
    kk = kk / jnp.maximum(jnp.sqrt(_head_sum(kk * kk, hm_ref[...])), 1e-12)
    kd = k * (1.0 + (a - 1.0) * ka_ref[...])
    bb = kk * a
    scan_one_chunk()
    yield "prep"
    cums = [sum(jnp.dot(tri, p, preferred_element_type=F32) for p in _split(logw[rows[c]], 2))
            for c in range(nchunk)]
    scan_one_chunk()
    yield "prep"
    dec = []
    for c in range(nchunk):
        cum, lw = cums[c], logw[rows[c]]
        cmid = cum[CHUNK // 2:CHUNK // 2 + 1]
        ctot = cum[0:1] if reverse else cum[CHUNK - 1:CHUNK]
        e1 = jnp.exp(cum - cmid)
        e1i = jnp.exp(cmid - cum)
        ee = jnp.exp(ctot - cum)
        emid = jnp.exp(cmid)
        r_g = r[rows[c]] * e1
        a_g = -kk[rows[c]] * (e1 * jnp.exp(-lw))
        dec.append(dict(
            r_g=r_g, a_g=a_g, b_g=bb[rows[c]] * e1i, k_g=kd[rows[c]] * e1i, r_s=r_g * emid,
            a_s=a_g * emid, b_e=bb[rows[c]] * ee, k_e=kd[rows[c]] * ee, v=v[rows[c]],
            wtot=jnp.exp(ctot)))
        yield "prep"

    tr = {}
    for name in ("b_g", "k_g", "b_e", "k_e"):
        for c in range(0, nchunk, 2):
            for q in range(ngroup):
                tr[name, c, q], tr[name, c + 1, q] = _head_transpose_pair(
                    dec[c][name][:, lanes[q]], dec[c + 1][name][:, lanes[q]], low)
        yield "prep"
    yield "prep_done"

    st_v = [st(dec[c]["v"][:, lanes[q]]) for c, q in chains]
    grams = [_bdot(jnp.concatenate([dec[c]["a_g"][:, lanes[q]], dec[c]["r_g"][:, lanes[q]]], axis=0),
                   jnp.concatenate([st(tr["b_g", c, q]), st(tr["k_g", c, q])], axis=1))
             for c, q in chains]
    scan_one_chunk()
    yield "stage"
    a_ab = [jnp.where(strict, g_[0:CHUNK, 0:MXU_DIM], 0.0) for g_ in grams]
    a_kr = [jnp.concatenate([jnp.where(strict, g_[0:CHUNK, MXU_DIM:], 0.0),
                             jnp.where(incl, g_[CHUNK:, MXU_DIM:], 0.0)], axis=0) for g_ in grams]
    a_rb = [jnp.where(incl, g_[CHUNK:, 0:MXU_DIM], 0.0) for g_ in grams]
    tinv = [eye_lc + a_ for a_ in a_ab]
    apow = [_bdot(a_, st(a_)) for a_ in a_ab]
    scan_one_chunk()
    yield "stage"
    npow = CHUNK.bit_length() - 2
    for jj in range(npow):
        if jj < npow - 1:
            both = [_bdot(jnp.concatenate([ap, ti_], axis=0), st(ap)) for ap, ti_ in zip(apow, tinv)]
            apow = [b_[0:CHUNK] for b_ in both]
            tinv = [ti_ + b_[CHUNK:] for ti_, b_ in zip(tinv, both)]
        else:
            tinv = [ti_ + _bdot(ti_, st(ap)) for ap, ti_ in zip(apow, tinv)]
        scan_one_chunk()
        yield "stage"
    xyq = [_bdot(jnp.concatenate([a_kr[i], tr["k_e", c, q]], axis=0), st_v[i])
           for i, (c, q) in enumerate(chains)]
    xy = [x[0:2 * CHUNK] for x in xyq]
    q2 = [x[2 * CHUNK:] for x in xyq]
    yield "stage"
    au = [_bdot(tinv[i], jnp.concatenate([st(dec[c]["a_s"][:, lanes[q]]), st(xy[i][0:CHUNK])], axis=1))
          for i, (c, q) in enumerate(chains)]
    while pending:
        scan_one_chunk()
    yield "stage"
    st_au = [jnp.concatenate([st(x[:, 0:MXU_DIM]), st(x[:, MXU_DIM:])], axis=1) for x in au]
    rypq = [_bdot(jnp.concatenate([a_rb[i], tr["b_e", c, q]], axis=0), st_au[i])
            for i, (c, q) in enumerate(chains)]
    ry = [x[0:CHUNK] for x in rypq]
    pq = [x[CHUNK:] for x in rypq]
    for q in range(ngroup):
        z_ref[q] = zs[q]
    for i, (c, q) in enumerate(chains):
        r_hat = dec[c]["r_s"][:, lanes[q]] + ry[i][:, 0:MXU_DIM]
        p_lc = eye_lc * dec[c]["wtot"][:, lanes[q]] + pq[i][:, 0:MXU_DIM]
        rp_ref[c, q] = jnp.concatenate([r_hat, p_lc], axis=0).astype(BF16)
        yh_ref[c, q] = ry[i][:, MXU_DIM:] + xy[i][CHUNK:]
        ql_ref[c, q] = pq[i][:, MXU_DIM:] + q2[i]
    yield "stage"


_WKV_DIR_INPUTS = 9
_WKV_DIR_SCRATCH = 4


def _wkv_pair_kernel(*refs, nsteps):
    ni, ns = _WKV_DIR_INPUTS, _WKV_DIR_SCRATCH
    fwd_in, bwd_in, shared = refs[0:ni], refs[ni:2 * ni], refs[2 * ni:2 * ni + 3]
    y_fwd, y_bwd = refs[2 * ni + 3:2 * ni + 5]
    scratch = refs[2 * ni + 5:]
    g = pl.program_id(0)

    @pl.when(g == 0)
    def _():
        for ref in scratch:
            ref[...] = jnp.zeros_like(ref)

    fwd = _wkv_program(g, *fwd_in, *shared, y_fwd, *scratch[0:ns], reverse=False, nsteps=nsteps)
    bwd = _wkv_program(g, *bwd_in, *shared, y_bwd, *scratch[ns:2 * ns], reverse=True, nsteps=nsteps)
    for tag in fwd:
        if tag == "prep_done":
            break
    bwd_preparing = True
    for _ in fwd:
        if bwd_preparing:
            bwd_preparing = next(bwd) != "prep_done"
    while bwd_preparing:
        bwd_preparing = next(bwd) != "prep_done"
    for _ in bwd:
        pass


_UM_WL_BLOCK = 3 * RW_WIDTH // LANES
_UM_AL_BLOCK = _UM_WL_BLOCK + 1
_UM_GL_BLOCK = _UM_WL_BLOCK + 2


def _wkv_pair(um, d_fwd, d_bwd):
    b, t, _ = um.shape
    ts = min(WKV_STEP, t)
    nsteps = t // ts
    nblocks = b * nsteps
    nchunk = ts // CHUNK
    ngroup = RW_WIDTH // MXU_DIM
    const = lambda g: (0, 0)
    vec = pl.BlockSpec((1, RW_WIDTH), const)

    def dir_specs(reverse):
        pos = (lambda si: nsteps - 1 - si) if reverse else (lambda si: si)
        block_of = lambda gi, col: (gi // nsteps, pos(gi % nsteps), col)
        wide = lambda col: pl.BlockSpec((None, ts, RW_WIDTH),
                                        lambda g: block_of(jnp.minimum(g, nblocks - 1), col))
        narrow = lambda col: pl.BlockSpec((None, ts, LANES),
                                          lambda g: block_of(jnp.minimum(g, nblocks - 1), col))
        ins = [wide(0), wide(1), wide(2), narrow(_UM_WL_BLOCK), narrow(_UM_AL_BLOCK), vec,
               pl.BlockSpec((2 * DECAY_LORA, RW_WIDTH), const), vec,
               pl.BlockSpec((2 * AAA_LORA, RW_WIDTH), const)]
        out = pl.BlockSpec((None, ts, RW_WIDTH), lambda g: block_of(jnp.maximum(g - 1, 0), 0))
        return ins, out

    dir_args = lambda wp: [um, um, um, um, um, wp["w0"], wp["wup"], wp["a0"], wp["aup"]]
    fwd_ins, fwd_out = dir_specs(False)
    bwd_ins, bwd_out = dir_specs(True)
    dir_scratch = [pltpu.VMEM((ngroup, HEAD_DIM, MXU_DIM), F32),
                   pltpu.VMEM((nchunk, ngroup, 2 * CHUNK, MXU_DIM), BF16),
                   pltpu.VMEM((nchunk, ngroup, CHUNK, MXU_DIM), F32),
                   pltpu.VMEM((nchunk, ngroup, HEAD_DIM, MXU_DIM), F32)]
    shp = jax.ShapeDtypeStruct((b, t, RW_WIDTH), F32)
    return pl.pallas_call(
        functools.partial(_wkv_pair_kernel, nsteps=nsteps),
        grid=(nblocks + 1,),
        in_specs=fwd_ins + bwd_ins + [vec, vec, pl.BlockSpec((MXU_DIM, MXU_DIM), const)],
        out_specs=[fwd_out, bwd_out],
        out_shape=[shp, shp],
        scratch_shapes=dir_scratch + dir_scratch,
        compiler_params=_params("arbitrary"),
        name="wkv",
    )(*dir_args(d_fwd), *dir_args(d_bwd), d_fwd["k_k"], d_fwd["k_a"], d_fwd["hm"])


def _outproj_kernel(h_ref, yna_ref, y0_ref, y1_ref, r_ref, k_ref, v_ref, al_ref, gl_ref,
                    a00_ref, aup0_ref, a01_ref, aup1_ref, ka_ref, rk_ref, gup_ref, lw_ref, lb_ref,
                    hm_ref, w_ref, o_ref):
    hm = hm_ref[...]
    half = h_ref.shape[0] // 2
    groups = [slice(0, half), slice(half, 2 * half)]
    wkv = [y0_ref[rows] + y1_ref[rows] for rows in groups]
    mean = [_head_sum(x, hm) * (1.0 / HEAD_DIM) for x in wkv]
    dev = [x - m_ for x, m_ in zip(wkv, mean)]
    var = [_head_sum(d_ * d_, hm) * (1.0 / HEAD_DIM) for d_ in dev]
    yn = [d_ * lax.rsqrt(v_ + LNX_EPS) * lw_ref[...] + lb_ref[...] for d_, v_ in zip(dev, var)]
    a_sum = [_sigmoid(a00_ref[...] + _bdot(al_ref[rows], aup0_ref[...]))
             + _sigmoid(a01_ref[...] + _bdot(al_ref[rows], aup1_ref[...])) for rows in groups]
    kd_sum = [k_ref[rows] * (2.0 + (a_ - 2.0) * ka_ref[...]) for rows, a_ in zip(groups, a_sum)]
    bonus = [_head_sum(r_ref[rows] * kd * rk_ref[...], hm) * v_ref[rows] for rows, kd in zip(groups, kd_sum)]
    gate = [_bdot(_sigmoid(gl_ref[rows]), gup_ref[...]) for rows in groups]
    for rows, yn_, bonus_, gate_ in zip(groups, yn, bonus, gate):
        y_rw = (yn_ + bonus_) * gate_
        mix = jnp.concatenate([yna_ref[rows], y_rw.astype(BF16)], axis=1)
        o_ref[rows] = h_ref[rows] + jnp.dot(mix, w_ref[...], preferred_element_type=F32)


def _outproj(h, yna, y0, y1, um, lw):
    m, d = h.shape
    tm = min(ROW_TILE, m)
    row = lambda i: (i, 0)
    const = lambda i: (0, 0)
    tok = pl.BlockSpec((tm, RW_WIDTH), row)
    vec = pl.BlockSpec((1, RW_WIDTH), const)
    um_wide = lambda col: pl.BlockSpec((tm, RW_WIDTH), lambda i: (i, col))
    um_narrow = lambda col: pl.BlockSpec((tm, LANES), lambda i: (i, col))
    lora = pl.BlockSpec((2 * AAA_LORA, RW_WIDTH), const)
    d0, d1 = lw["dirs"]
    return pl.pallas_call(
        _outproj_kernel,
        grid=(m // tm,),
        in_specs=[pl.BlockSpec((tm, d), row), pl.BlockSpec((tm, NA_WIDTH), row), tok, tok,
                  um_wide(0), um_wide(1), um_wide(2), um_narrow(_UM_AL_BLOCK), um_narrow(_UM_GL_BLOCK),
                  vec, lora, vec, lora, vec, vec, pl.BlockSpec((GATE_LORA, RW_WIDTH), const), vec, vec,
                  pl.BlockSpec((MXU_DIM, MXU_DIM), const),
                  pl.BlockSpec((NA_WIDTH + RW_WIDTH, d), const)],
        out_specs=pl.BlockSpec((tm, d), row),
        out_shape=jax.ShapeDtypeStruct((m, d), F32),
        compiler_params=_params("parallel"),
        name="outproj",
    )(h, yna, y0, y1, um, um, um, um, um, d0["a0"], d0["aup"], d1["a0"], d1["aup"], d0["k_a"],
      lw["r_k"], lw["gup"], lw["lnx_w"], lw["lnx_b"], lw["hm"], lw["w_out"])


def _pad_lora(w_up):
    zero = jnp.zeros_like(w_up[0])
    return (jnp.concatenate([w_up[0], zero], axis=0).astype(BF16),
            jnp.concatenate([zero, w_up[1]], axis=0).astype(BF16))


def _layer_weights(w, i):
    row = lambda x: x.reshape(1, -1)
    wup = _pad_lora(w["rw_w_up"][i])
    aup = _pad_lora(w["rw_a_up"][i])
    lane = jnp.arange(MXU_DIM)
    hm = (lane[:, None] // HEAD_DIM == lane[None, :] // HEAD_DIM).astype(BF16)
    common = dict(k_k=row(w["rw_k_k"][i]), k_a=row(w["rw_k_a"][i]), hm=hm)
    dirs = [dict(common, w0=row(w["rw_w0"][i, d]), wup=wup[d], a0=row(w["rw_a0"][i, d]), aup=aup[d])
            for d in range(2)]
    w_in = w["w_in"][i].astype(BF16)
    return dict(
        ffn1=(row(w["ffn1_norm"][i]), w["ffn1_wg"][i].astype(BF16), w["ffn1_wu"][i].astype(BF16),
              w["ffn1_wd"][i].astype(BF16)),
        ffn2=(row(w["ffn2_norm"][i]), w["ffn2_wg"][i].astype(BF16), w["ffn2_wu"][i].astype(BF16),
              w["ffn2_wd"][i].astype(BF16)),
        mix_norm=row(w["mix_norm"][i]), w_qkv=w_in[:, :3 * NA_WIDTH], w_rw=w_in[:, 3 * NA_WIDTH:],
        mu=row(w["rw_mu"][i]), tz=_na_bias_table(w["na_rpb"][i]), dirs=dirs, hm=hm,
        r_k=row(w["rw_r_k"][i]), gup=w["rw_g_up"][i].astype(BF16),
        lnx_w=row(w["rw_lnx_w"][i]), lnx_b=row(w["rw_lnx_b"][i]), w_out=w["w_out"][i].astype(BF16),
        ple=(row(w["ple_norm"][i]), w["ple_gate"][i].astype(BF16), w["ple_up"][i].astype(BF16)),
    )


def _trunk(x, p, layers, final_norm):
    b, t, d = x.shape
    m = b * t
    h = x.reshape(m, d)
    depth = len(layers)
    p_rows = p.reshape(depth, m, -1)
    for i, lw in enumerate(layers):
        h = _ffn(h, *lw["ffn1"])
        qkv, um = _inproj(h, lw["mix_norm"], lw["w_qkv"], lw["w_rw"], lw["mu"], seq_len=t)
        yna = _na(qkv.reshape(b, t, -1), lw["tz"])
        um3 = um.reshape(b, t, -1)
        y0, y1 = _wkv_pair(um3, lw["dirs"][0], lw["dirs"][1])
        h = _outproj(h, yna.reshape(m, -1), y0.reshape(m, -1), y1.reshape(m, -1), um, lw)
        pn, pg, pu = lw["ple"]
        h = _ffn(h, *lw["ffn2"], ple=(p_rows, i, pn, pg, pu),
                 final_norm=final_norm if i == depth - 1 else None)
    return h.reshape(b, t, d)


def kernel(x_prompt, x_sample, p_prompt, p_sample, ffn1_norm, ffn1_wg, ffn1_wu, ffn1_wd, mix_norm, w_in, na_rpb, rw_mu, rw_w0, rw_w_up, rw_a0, rw_a_up, rw_g_up, rw_k_k, rw_k_a, rw_r_k, rw_lnx_w, rw_lnx_b, w_out, ffn2_norm, ffn2_wg, ffn2_wu, ffn2_wd, ple_norm, ple_gate, ple_up, final_norm):
    w = dict(ffn1_norm=ffn1_norm, ffn1_wg=ffn1_wg, ffn1_wu=ffn1_wu, ffn1_wd=ffn1_wd, mix_norm=mix_norm,
             w_in=w_in, na_rpb=na_rpb, rw_mu=rw_mu, rw_w0=rw_w0, rw_w_up=rw_w_up, rw_a0=rw_a0,
             rw_a_up=rw_a_up, rw_g_up=rw_g_up, rw_k_k=rw_k_k, rw_k_a=rw_k_a, rw_r_k=rw_r_k,
             rw_lnx_w=rw_lnx_w, rw_lnx_b=rw_lnx_b, w_out=w_out, ffn2_norm=ffn2_norm, ffn2_wg=ffn2_wg,
             ffn2_wu=ffn2_wu, ffn2_wd=ffn2_wd, ple_norm=ple_norm, ple_gate=ple_gate, ple_up=ple_up)
    layers = [_layer_weights(w, i) for i in range(ffn1_wg.shape[0])]
    fn = final_norm.reshape(1, -1)
    return (_trunk(x_prompt, p_prompt, layers, fn), _trunk(x_sample, p_sample, layers, fn))
```
